```python
import math
import jax, jax.numpy as jnp
from jax import lax
import numpy as np

D_MODEL = 2048
BATCH = 4
SEQ = 2048
DEPTH = 1
DEC_BATCH = 4
DEC_SEQ = 8192
PAST_LEN = 128

N_MEM = 256
D_MIX = D_MODEL
EPS = 1e-6
SSD_WIDTH = D_MIX // 2
SSD_HEAD_DIM = 64
SSD_HEADS = SSD_WIDTH // SSD_HEAD_DIM
SSD_GROUPS = 2
SSD_STATE = 128
SSD_CHUNK = 128
D_CONV = 5
CONV_CH = SSD_WIDTH + 2 * SSD_GROUPS * SSD_STATE
DT_MIN = 1e-3
DT_MAX = 1e-1
ATT_WIDTH = D_MIX // 4
ATT_HEAD_DIM = 64
ATT_HEADS = ATT_WIDTH // ATT_HEAD_DIM
ATT_KV_HEADS = 2
WINDOW = 128
BLOCK = 128
MEM_WIDTH = D_MIX - SSD_WIDTH - ATT_WIDTH
MEM_HEADS = 4
MEM_HEAD_DIM = MEM_WIDTH // MEM_HEADS
IN_SIZES = (CONV_CH, SSD_WIDTH, 2 * SSD_HEADS,
            ATT_HEADS * ATT_HEAD_DIM, ATT_KV_HEADS * ATT_HEAD_DIM, ATT_KV_HEADS * ATT_HEAD_DIM, ATT_WIDTH,
            MEM_WIDTH, MEM_WIDTH)
D_IN = sum(IN_SIZES)

kernel_name = 'hymba_bidir_ssd_swa_mem_encoder'


def rms_norm(x, g):
    xf = x.astype(jnp.float32)
    y = xf * lax.rsqrt(jnp.mean(xf * xf, axis=-1, keepdims=True) + EPS)
    return (y * g.astype(jnp.float32)).astype(x.dtype)


def split_cols(x, sizes):
    idx = np.cumsum(sizes)[:-1].tolist()
    return jnp.split(x, idx, axis=-1)


def centred_dwconv(u, w, b):
    K = w.shape[0]
    out = lax.conv_general_dilated(u, w[:, None, :].astype(u.dtype), window_strides=(1,),
                                   padding=[((K - 1) // 2, K // 2)],
                                   dimension_numbers=('NWC', 'WIO', 'NWC'),
                                   feature_group_count=u.shape[-1])
    return out + b.astype(u.dtype)


def ssd_scan(x, dt, A, Bm, Cm):
    b, T, h, p = x.shape
    g, n = Bm.shape[2], Bm.shape[3]
    r = h // g
    L = SSD_CHUNK
    c = T // L
    f32 = jnp.float32
    xs = x.astype(f32).reshape(b, c, L, g, r, p)
    dt = dt.astype(f32).reshape(b, c, L, g, r)
    Bm = Bm.astype(f32).reshape(b, c, L, g, n)
    Cm = Cm.astype(f32).reshape(b, c, L, g, n)
    acs = jnp.cumsum(dt * A.astype(f32).reshape(g, r), axis=2)
    xdt = xs * dt[..., None]
    seg = acs[:, :, :, None] - acs[:, :, None, :]
    tril = jnp.tril(jnp.ones((L, L), dtype=bool))[:, :, None, None]
    decay = jnp.exp(jnp.where(tril, seg, -jnp.inf))
    cb = jnp.einsum('bcign,bcjgn->bcijg', Cm, Bm)
    y_diag = jnp.einsum('bcijgr,bcjgrp->bcigrp', cb[..., None] * decay, xdt)
    decay_end = jnp.exp(acs[:, :, -1:] - acs)
    states = jnp.einsum('bcjgn,bcjgrp->bcgrpn', Bm, xdt * decay_end[..., None])
    chunk_decay = jnp.exp(acs[:, :, -1])

    def step(S, inp):
        dA, st = inp
        return dA[..., None, None] * S + st, S

    S0 = jnp.zeros((b, g, r, p, n), f32)
    _, S_prev = lax.scan(step, S0, (jnp.moveaxis(chunk_decay, 1, 0), jnp.moveaxis(states, 1, 0)))
    S_prev = jnp.moveaxis(S_prev, 0, 1)
    y_off = jnp.einsum('bcign,bcgrpn->bcigrp', Cm, S_prev) * jnp.exp(acs)[..., None]
    return (y_diag + y_off).reshape(b, T, h, p).astype(x.dtype)


def alibi_slopes(n_heads):
    return 2.0 ** (-8.0 * jnp.arange(1, n_heads + 1, dtype=jnp.float32) / n_heads)


def window_attention(q, k, v, sink):
    b, T, H, d = q.shape
    KV = k.shape[2]
    r = H // KV
    nb = T // BLOCK
    qb = q.reshape(b, nb, BLOCK, KV, r, d)
    pad = ((0, 0), (BLOCK, BLOCK), (0, 0), (0, 0))
    kp = jnp.pad(k, pad).reshape(b, nb + 2, BLOCK, KV, d)
    vp = jnp.pad(v, pad).reshape(b, nb + 2, BLOCK, KV, d)
    kw = jnp.concatenate([kp[:, :-2], kp[:, 1:-1], kp[:, 2:]], axis=2)
    vw = jnp.concatenate([vp[:, :-2], vp[:, 1:-1], vp[:, 2:]], axis=2)
    s = jnp.einsum('bnqgrd,bnkgd->bngrqk', qb, kw).astype(jnp.float32) * (d ** -0.5)
    blk = jnp.arange(nb)[:, None] * BLOCK
    qpos = blk + jnp.arange(BLOCK)[None, :]
    kpos = blk - BLOCK + jnp.arange(3 * BLOCK)[None, :]
    dist = jnp.abs(qpos[:, :, None] - kpos[:, None, :]).astype(jnp.float32)
    valid = (dist <= WINDOW) & ((kpos >= 0) & (kpos < T))[:, None, :]
    slopes = alibi_slopes(H).reshape(KV, r)
    logits = jnp.where(valid[:, None, None], s - slopes[None, :, :, None, None] * dist[:, None, None], -jnp.inf)
    sk = sink.astype(jnp.float32).reshape(KV, r)[:, :, None, None]
    m = jnp.maximum(jnp.max(logits, axis=-1, keepdims=True), sk)
    pr = jnp.exp(logits - m)
    probs = pr / (jnp.sum(pr, axis=-1, keepdims=True) + jnp.exp(sk - m))
    o = jnp.einsum('bngrqk,bnkgd->bnqgrd', probs.astype(v.dtype), vw)
    return o.reshape(b, T, H * d)


def layer(x, mem, norm_g, w_in, conv_w, conv_b, dt_bias, a_log, d_skip, ssd_norm_g,
          q_norm_g, k_norm_g, sink, mem_norm_g, w_mem_kv, mq_norm_g, mk_norm_g, w_out):
    b, T, _ = x.shape
    hn = rms_norm(x, norm_g)
    proj = hn @ w_in
    xbc, z_ssd, dt_raw, q, k, v, z_att, mq, z_mem = split_cols(proj, IN_SIZES)

    xbc = jax.nn.silu(centred_dwconv(xbc, conv_w, conv_b))
    xs, Bm, Cm = split_cols(xbc, (SSD_WIDTH, SSD_GROUPS * SSD_STATE, SSD_GROUPS * SSD_STATE))
    xs = xs.reshape(b, T, SSD_HEADS, SSD_HEAD_DIM)
    Bm = Bm.reshape(b, T, SSD_GROUPS, SSD_STATE)
    Cm = Cm.reshape(b, T, SSD_GROUPS, SSD_STATE)
    dt = jax.nn.softplus(dt_raw.astype(jnp.float32).reshape(b, T, 2, SSD_HEADS) + dt_bias.astype(jnp.float32))
    A = -jnp.exp(a_log.astype(jnp.float32))
    y_f = ssd_scan(xs, dt[:, :, 0], A[0], Bm, Cm)
    y_b = ssd_scan(xs[:, ::-1], dt[:, ::-1, 1], A[1], Bm[:, ::-1], Cm[:, ::-1])[:, ::-1]
    y = (y_f + y_b + d_skip[:, None] * xs).reshape(b, T, SSD_WIDTH)
    yg = (y * jax.nn.silu(z_ssd)).reshape(b, T, SSD_GROUPS, SSD_WIDTH // SSD_GROUPS)
    o_ssd = rms_norm(yg, ssd_norm_g.reshape(SSD_GROUPS, SSD_WIDTH // SSD_GROUPS)).reshape(b, T, SSD_WIDTH)

    q = rms_norm(q.reshape(b, T, ATT_HEADS, ATT_HEAD_DIM), q_norm_g)
    k = rms_norm(k.reshape(b, T, ATT_KV_HEADS, ATT_HEAD_DIM), k_norm_g)
    v = v.reshape(b, T, ATT_KV_HEADS, ATT_HEAD_DIM)
    o_att = window_attention(q, k, v, sink) * jax.nn.silu(z_att)

    memn = rms_norm(mem, mem_norm_g)
    mk, mv = split_cols(memn @ w_mem_kv, (MEM_WIDTH, MEM_WIDTH))
    M = mem.shape[1]
    mk = rms_norm(mk.reshape(b, M, MEM_HEADS, MEM_HEAD_DIM), mk_norm_g)
    mv = mv.reshape(b, M, MEM_HEADS, MEM_HEAD_DIM)
    mq = rms_norm(mq.reshape(b, T, MEM_HEADS, MEM_HEAD_DIM), mq_norm_g)
    sm = jnp.einsum('bthd,bmhd->bhtm', mq, mk).astype(jnp.float32) * (MEM_HEAD_DIM ** -0.5)
    pm = jax.nn.softmax(sm, axis=-1).astype(mv.dtype)
    o_mem = jnp.einsum('bhtm,bmhd->bthd', pm, mv).reshape(b, T, MEM_WIDTH) * jax.nn.silu(z_mem)

    return x + jnp.concatenate([o_ssd, o_att, o_mem], axis=-1) @ w_out


def trunk(x, mem, params):
    for l in range(DEPTH):
        x = layer(x, mem, *[p[l] for p in params])
    return x


def setup_inputs(seed: int = 0) -> dict:
    key = jax.random.key(seed)
    ks = jax.random.split(key, 24)
    f32 = jnp.float32

    def nrm(k, shape, scale):
        return jax.random.normal(k, shape, f32) * scale

    u = jax.random.uniform(ks[8], (DEPTH, 2, SSD_HEADS), f32)
    dt0 = jnp.exp(u * (math.log(DT_MAX) - math.log(DT_MIN)) + math.log(DT_MIN))
    return {
        'x_prompt': nrm(ks[0], (BATCH, SEQ, D_MODEL), 1.0),
        'x_sample': nrm(ks[1], (DEC_BATCH, DEC_SEQ, D_MODEL), 1.0),
        'mem_prompt': nrm(ks[2], (BATCH, N_MEM, D_MODEL), 1.0),
        'mem_sample': nrm(ks[3], (DEC_BATCH, N_MEM, D_MODEL), 1.0),
        'norm_g': 1.0 + nrm(ks[4], (DEPTH, D_MODEL), 0.02),
        'w_in': nrm(ks[5], (DEPTH, D_MODEL, D_IN), D_MODEL ** -0.5),
        'conv_w': nrm(ks[6], (DEPTH, D_CONV, CONV_CH), D_CONV ** -0.5),
        'conv_b': nrm(ks[7], (DEPTH, CONV_CH), 0.02),
        'dt_bias': dt0 + jnp.log(-jnp.expm1(-dt0)),
        'a_log': jnp.log(jax.random.uniform(ks[9], (DEPTH, 2, SSD_HEADS), f32, minval=1.0, maxval=16.0)),
        'd_skip': 1.0 + nrm(ks[10], (DEPTH, SSD_HEADS), 0.02),
        'ssd_norm_g': 1.0 + nrm(ks[11], (DEPTH, SSD_WIDTH), 0.02),
        'q_norm_g': 1.0 + nrm(ks[12], (DEPTH, ATT_HEAD_DIM), 0.02),
        'k_norm_g': 1.0 + nrm(ks[13], (DEPTH, ATT_HEAD_DIM), 0.02),
        'sink': nrm(ks[14], (DEPTH, ATT_HEADS), 0.5),
        'mem_norm_g': 1.0 + nrm(ks[15], (DEPTH, D_MODEL), 0.02),
        'w_mem_kv': nrm(ks[16], (DEPTH, D_MODEL, 2 * MEM_WIDTH), D_MODEL ** -0.5),
        'mq_norm_g': 1.0 + nrm(ks[17], (DEPTH, MEM_HEAD_DIM), 0.02),
        'mk_norm_g': 1.0 + nrm(ks[18], (DEPTH, MEM_HEAD_DIM), 0.02),
        'w_out': nrm(ks[19], (DEPTH, D_MIX, D_MODEL), D_MIX ** -0.5),
    }


def reference(x_prompt, x_sample, mem_prompt, mem_sample, norm_g, w_in, conv_w, conv_b, dt_bias,
              a_log, d_skip, ssd_norm_g, q_norm_g, k_norm_g, sink, mem_norm_g, w_mem_kv,
              mq_norm_g, mk_norm_g, w_out):
    params = (norm_g, w_in, conv_w, conv_b, dt_bias, a_log, d_skip, ssd_norm_g, q_norm_g, k_norm_g,
              sink, mem_norm_g, w_mem_kv, mq_norm_g, mk_norm_g, w_out)
    y_prompt = trunk(x_prompt, mem_prompt, params)
    y_sample = trunk(x_sample, mem_sample, params)
    return (y_prompt, y_sample)
```

```python
import functools

import jax
import jax.numpy as jnp
from jax import lax
from jax.experimental import pallas as pl
from jax.experimental.pallas import tpu as pltpu

F32 = jnp.float32
BF16 = jnp.bfloat16

D_MODEL = 2048
N_MEM = 256
EPS = 1e-6
SSD_WIDTH = 1024
SSD_HEAD_DIM = 64
SSD_HEADS = 16
SSD_GROUPS = 2
SSD_STATE = 128
CHUNK = 128
D_CONV = 5
ATT_WIDTH = 512
ATT_HEAD_DIM = 64
ATT_HEADS = 8
ATT_KV_HEADS = 2
WINDOW = 128
MEM_WIDTH = 512
MEM_HEADS = 4
MEM_HEAD_DIM = 128
BC_WIDTH = 2 * SSD_GROUPS * SSD_STATE
KV_WIDTH = ATT_KV_HEADS * ATT_HEAD_DIM
GROUP_COLS = SSD_WIDTH // SSD_GROUPS
LANES = 128
HALO = 16
ROW_TILE = 512
ATT_HEAD_ORDER = (0, 4, 1, 5, 2, 6, 3, 7)
ALIBI_SLOPES = tuple(2.0 ** (-8.0 * (h + 1) / ATT_HEADS) for h in range(ATT_HEADS))
VMEM_LIMIT_BIG = 52 * 1024 * 1024
VMEM_LIMIT_SMALL = 40 * 1024 * 1024


def _silu(x):
    return x * jax.nn.sigmoid(x)


def _dot(a, b):
    return jnp.dot(a, b, preferred_element_type=F32)


def _dot_nt(a, b):
    return lax.dot_general(a, b, (((1,), (1,)), ((), ())), preferred_element_type=F32)


def _split2(v):
    hi = v.astype(BF16)
    lo = (v - hi.astype(F32)).astype(BF16)
    return hi, lo


def _split3(v):
    hi = v.astype(BF16)
    r1 = v - hi.astype(F32)
    mid = r1.astype(BF16)
    lo = (r1 - mid.astype(F32)).astype(BF16)
    return hi, mid, lo


def _dot_right_exact(v, m):
    hi, lo = _split2(v)
    return _dot(hi, m) + _dot(lo, m)


def _dot_left_exact(m, v):
    hi, mid, lo = _split3(v)
    return _dot(m, hi) + _dot(m, mid) + _dot(m, lo)


def _group_rms(v, gmat):
    ms = _dot_right_exact(v * v, gmat)
    return v * lax.rsqrt(ms + EPS)


def _softplus(x):
    return jnp.maximum(x, 0.0) + jnp.log1p(jnp.exp(-jnp.abs(x)))


def _neg_exp_alog(alog_row):
    lane = lax.broadcasted_iota(jnp.int32, alog_row.shape, 1)
    return jnp.where(lane < 2 * SSD_HEADS, -jnp.exp(alog_row), 0.0)


def _inproj_kernel(x_ref, ng_ref, wxs, wbc, wzs, wq, wk, wv, wza, wmq, wzm, wdt,
                   gq_ref, gk_ref, gmq_ref, g64q_ref, g64k_ref, g128_ref, dtb_ref,
                   o_xs, o_bc, o_zs, o_q, o_k, o_v, o_za, o_mq, o_zm, o_dt):
    x = x_ref[...]
    ms = jnp.mean(x * x, axis=-1, keepdims=True)
    hn = (x * lax.rsqrt(ms + EPS) * ng_ref[...]).astype(BF16)

    o_xs[...] = _dot(hn, wxs[...]).astype(BF16)
    o_bc[...] = _dot(hn, wbc[...]).astype(BF16)
    o_zs[...] = _silu(_dot(hn, wzs[...])).astype(BF16)
    q = _group_rms(_dot(hn, wq[...]), g64q_ref[...])
    o_q[...] = (q * gq_ref[...] * (ATT_HEAD_DIM ** -0.5)).astype(BF16)
    k = _group_rms(_dot(hn, wk[...]), g64k_ref[...])
    o_k[...] = (k * gk_ref[...]).astype(BF16)
    o_v[...] = _dot(hn, wv[...]).astype(BF16)
    o_za[...] = _silu(_dot(hn, wza[...])).astype(BF16)
    mq = _group_rms(_dot(hn, wmq[...]), g128_ref[...])
    o_mq[...] = (mq * gmq_ref[...] * (MEM_HEAD_DIM ** -0.5)).astype(BF16)
    o_zm[...] = _silu(_dot(hn, wzm[...])).astype(BF16)
    o_dt[...] = _softplus(_dot(hn, wdt[...]) + dtb_ref[...])


def _const_spec(shape):
    nd = len(shape)
    return pl.BlockSpec(shape, lambda *_: (0,) * nd, pipeline_mode=pl.Buffered(1))


def _inproj(x2d, p):
    rows = x2d.shape[0]
    assert rows % ROW_TILE == 0
    widths = (SSD_WIDTH, BC_WIDTH, SSD_WIDTH, ATT_WIDTH, KV_WIDTH, KV_WIDTH, ATT_WIDTH, MEM_WIDTH, MEM_WIDTH)
    weights = (p["w_xs"], p["w_bc"], p["w_zs"], p["w_q"], p["w_k"], p["w_v"], p["w_za"], p["w_mq"], p["w_zm"],
               p["w_dt"])
    consts = (p["gq"], p["gk"], p["gmq"], p["g64q"], p["g64k"], p["g128"], p["dt_bias"])
    row_spec = lambda w: pl.BlockSpec((ROW_TILE, w), lambda i: (i, 0))
    out_shape = tuple(jax.ShapeDtypeStruct((rows, w), BF16) for w in widths) + (
        jax.ShapeDtypeStruct((rows, LANES), F32),)
    return pl.pallas_call(
        _inproj_kernel,
        grid=(rows // ROW_TILE,),
        in_specs=[row_spec(D_MODEL), _const_spec((1, D_MODEL))]
        + [_const_spec(w.shape) for w in weights] + [_const_spec(c.shape) for c in consts],
        out_specs=tuple(row_spec(w) for w in widths) + (row_spec(LANES),),
        out_shape=out_shape,
        compiler_params=pltpu.CompilerParams(dimension_semantics=("arbitrary",),
                                             vmem_limit_bytes=VMEM_LIMIT_BIG),
        name="inproj",
    )(x2d, p["norm_g"], *weights, *consts)


def _memkv_kernel(mem_ref, g_ref, wk_ref, wv_ref, gmk_ref, g128_ref, mk_ref, mv_ref):
    m = mem_ref[0]
    ms = jnp.mean(m * m, axis=-1, keepdims=True)
    mn = (m * lax.rsqrt(ms + EPS) * g_ref[...]).astype(BF16)
    mk = _group_rms(_dot(mn, wk_ref[...]), g128_ref[...])
    mk_ref[0] = (mk * gmk_ref[...]).astype(BF16)
    mv_ref[0] = _dot(mn, wv_ref[...]).astype(BF16)


def _memkv(mem, p):
    b = mem.shape[0]
    out_spec = pl.BlockSpec((1, N_MEM, MEM_WIDTH), lambda i: (i, 0, 0))
    return pl.pallas_call(
        _memkv_kernel,
        grid=(b,),
        in_specs=[pl.BlockSpec((1, N_MEM, D_MODEL), lambda i: (i, 0, 0)), _const_spec((1, D_MODEL)),
                  _const_spec(p["w_mk"].shape), _const_spec(p["w_mv"].shape),
                  _const_spec((1, MEM_WIDTH)), _const_spec(p["g128"].shape)],
        out_specs=(out_spec, out_spec),
        out_shape=(jax.ShapeDtypeStruct((b, N_MEM, MEM_WIDTH), BF16),) * 2,
        compiler_params=pltpu.CompilerParams(dimension_semantics=("arbitrary",),
                                             vmem_limit_bytes=VMEM_LIMIT_SMALL),
        name="memkv",
    )(mem, p["mem_norm_g"], p["w_mk"], p["w_mv"], p["gmk"], p["g128"])


def _conv_state_kernel(xs_ref, xsp_ref, xsn_ref, bc_ref, bcp_ref, bcn_ref, dt_ref, alog_ref,
                       cwx_ref, cbx_ref, cwb_ref, cbb_ref, eb_ref,
                       xso_ref, bco_ref, sb_ref, ext_x, ext_b, state, *, n_chunks):
    j = pl.program_id(1)
    c = n_chunks - 1 - j

    @pl.when(j == 0)
    def _():
        state[...] = jnp.zeros_like(state)

    has_prev = c > 0
    has_next = c < n_chunks - 1

    def conv(cur_ref, p_ref, n_ref, ext, w_ref, b_ref):
        ext[0:HALO, :] = jnp.where(has_prev, p_ref[...].astype(F32), 0.0)
        ext[HALO:HALO + CHUNK, :] = cur_ref[...].astype(F32)
        ext[HALO + CHUNK:2 * HALO + CHUNK, :] = jnp.where(has_next, n_ref[...].astype(F32), 0.0)
        lo = HALO - (D_CONV - 1) // 2
        acc = b_ref[...] + ext[lo:lo + CHUNK, :] * w_ref[0:1, :]
        for k in range(1, D_CONV):
            acc = acc + ext[lo + k:lo + k + CHUNK, :] * w_ref[k:k + 1, :]
        return _silu(acc)

    xc = conv(xs_ref, xsp_ref, xsn_ref, ext_x, cwx_ref, cbx_ref)
    bcc = conv(bc_ref, bcp_ref, bcn_ref, ext_b, cwb_ref, cbb_ref)
    xso_ref[...] = xc.astype(BF16)
    bco_ref[...] = bcc.astype(BF16)

    dt = dt_ref[...]
    a = dt * _neg_exp_alog(alog_ref[...])
    row = lax.broadcasted_iota(jnp.int32, (CHUNK, CHUNK), 0)
    col = lax.broadcasted_iota(jnp.int32, (CHUNK, CHUNK), 1)
    strict_lower = jnp.where(row > col, 1.0, 0.0).astype(BF16)
    pre = _dot_left_exact(strict_lower, a)
    tot = jnp.sum(a, axis=0, keepdims=True)
    eb = eb_ref[...]
    w_exp = _dot_right_exact(dt * jnp.exp(pre), eb)
    cd = _dot_right_exact(jnp.broadcast_to(jnp.exp(tot), (8, LANES)), eb)[0:1]
    xw = (xc * w_exp).astype(BF16)
    for g in range(SSD_GROUPS):
        gs = slice(g * GROUP_COLS, (g + 1) * GROUP_COLS)
        sb_ref[0, g] = state[g]
        bt = bcc[:, g * SSD_STATE:(g + 1) * SSD_STATE].T.astype(BF16)
        state[g] = state[g] * cd[:, gs] + _dot(bt, xw[:, gs])


def _conv_state(xs_raw, bc_raw, dt, p, batch, seq):
    n_chunks = seq // CHUNK
    hb = CHUNK // HALO
    last_hb = seq // HALO - 1
    cur = lambda b, j: (b * n_chunks + (n_chunks - 1 - j), 0)
    prv = lambda b, j: (b * (seq // HALO) + jnp.maximum((n_chunks - 1 - j) * hb - 1, 0), 0)
    nxt = lambda b, j: (b * (seq // HALO) + jnp.minimum((n_chunks - j) * hb, last_hb), 0)
    rows = batch * seq
    state_shape = (SSD_GROUPS, SSD_STATE, GROUP_COLS)
    return pl.pallas_call(
        functools.partial(_conv_state_kernel, n_chunks=n_chunks),
        grid=(batch, n_chunks),
        in_specs=[pl.BlockSpec((CHUNK, SSD_WIDTH), cur), pl.BlockSpec((HALO, SSD_WIDTH), prv),
                  pl.BlockSpec((HALO, SSD_WIDTH), nxt),
                  pl.BlockSpec((CHUNK, BC_WIDTH), cur), pl.BlockSpec((HALO, BC_WIDTH), prv),
                  pl.BlockSpec((HALO, BC_WIDTH), nxt),
                  pl.BlockSpec((CHUNK, LANES), cur), _const_spec((1, LANES)),
                  _const_spec((8, SSD_WIDTH)), _const_spec((1, SSD_WIDTH)),
                  _const_spec((8, BC_WIDTH)), _const_spec((1, BC_WIDTH)),
                  _const_spec((LANES, SSD_WIDTH))],
        out_specs=(pl.BlockSpec((CHUNK, SSD_WIDTH), cur), pl.BlockSpec((CHUNK, BC_WIDTH), cur),
                   pl.BlockSpec((1,) + state_shape, lambda b, j: (b * n_chunks + (n_chunks - 1 - j), 0, 0, 0))),
        out_shape=(jax.ShapeDtypeStruct((rows, SSD_WIDTH), BF16), jax.ShapeDtypeStruct((rows, BC_WIDTH), BF16),
                   jax.ShapeDtypeStruct((batch * n_chunks,) + state_shape, F32)),
        scratch_shapes=[pltpu.VMEM((CHUNK + 2 * HALO, SSD_WIDTH), F32),
                        pltpu.VMEM((CHUNK + 2 * HALO, BC_WIDTH), F32),
                        pltpu.VMEM(state_shape, F32)],
        compiler_params=pltpu.CompilerParams(dimension_semantics=("arbitrary", "arbitrary"),
                                             vmem_limit_bytes=VMEM_LIMIT_SMALL),
        name="conv_state",
    )(xs_raw, xs_raw, xs_raw, bc_raw, bc_raw, bc_raw, dt, p["a_log"],
      p["conv_w_xs"], p["conv_b_xs"], p["conv_w_bc"], p["conv_b_bc"], p["e_bwd"])


def _mixer_kernel(xs_ref, bc_ref, zs_ref, dt_ref, q_ref, kp_ref, kc_ref, kn_ref, vp_ref, vc_ref, vn_ref,
                  za_ref, mq_ref, zm_ref, sb_ref, mk_ref, mv_ref, alog_ref, dskip_ref, ssdg_ref,
                  ef_ref, eb_ref, sink_ref, o_ref, state, *, n_chunks):
    c = pl.program_id(1)

    @pl.when(c == 0)
    def _():
        state[...] = jnp.zeros_like(state)

    neg_inf = -jnp.inf
    nh = SSD_HEADS

    dt = dt_ref[...]
    a = dt * _neg_exp_alog(alog_ref[...])
    row = lax.broadcasted_iota(jnp.int32, (CHUNK, CHUNK), 0)
    col = lax.broadcasted_iota(jnp.int32, (CHUNK, CHUNK), 1)
    tril = row >= col
    triu = row <= col
    lane = lax.broadcasted_iota(jnp.int32, (CHUNK, LANES), 1)
    acs = jnp.where(lane < nh,
                    _dot_left_exact(jnp.where(tril, 1.0, 0.0).astype(BF16), a),
                    _dot_left_exact(jnp.where(triu, 1.0, 0.0).astype(BF16), a))
    tot = jnp.sum(a, axis=0, keepdims=True)
    ea = jnp.exp(acs)
    w_end = dt * jnp.exp(tot - acs)
    acs_t = acs.T
    dt_t = dt.T
    ef = ef_ref[...]
    eb = eb_ref[...]
    ea_f = _dot_right_exact(ea, ef)
    ea_b = _dot_right_exact(ea, eb)
    w_f = _dot_right_exact(w_end, ef)
    cd_f = _dot_right_exact(jnp.broadcast_to(jnp.exp(tot), (8, LANES)), ef)[0:1]

    xs = xs_ref[...].astype(F32)
    bc_bf = bc_ref[...]
    lo_half = lane < SSD_HEAD_DIM

    for g in range(SSD_GROUPS):
        gs = slice(g * GROUP_COLS, (g + 1) * GROUP_COLS)
        b_g = bc_bf[:, g * SSD_STATE:(g + 1) * SSD_STATE]
        c_g = bc_bf[:, (SSD_GROUPS + g) * SSD_STATE:(SSD_GROUPS + g + 1) * SSD_STATE]
        cb = _dot_nt(c_g, b_g)
        y_pairs = []
        for pair in range(4):
            h0 = g * 8 + 2 * pair
            ms_ = []
            for h in (h0, h0 + 1):
                seg_f = acs[:, h:h + 1] - acs_t[h:h + 1, :]
                d_f = jnp.exp(jnp.where(tril, seg_f, neg_inf)) * dt_t[h:h + 1, :]
                seg_b = acs[:, nh + h:nh + h + 1] - acs_t[nh + h:nh + h + 1, :]
                d_b = jnp.exp(jnp.where(triu, seg_b, neg_inf)) * dt_t[nh + h:nh + h + 1, :]
                ms_.append((cb * (d_f + d_b)).astype(BF16))
            slab = xs[:, h0 * SSD_HEAD_DIM:(h0 + 2) * SSD_HEAD_DIM]
            rhs = jnp.concatenate([jnp.where(lo_half, slab, 0.0), jnp.where(lo_half, 0.0, slab)],
                                  axis=0).astype(BF16)
            y_pairs.append(_dot(jnp.concatenate(ms_, axis=1), rhs))
        y_diag = jnp.concatenate(y_pairs, axis=1)
        y_off_f = _dot(c_g, state[g].astype(BF16)) * ea_f[:, gs]
        y_off_b = _dot(c_g, sb_ref[0, g].astype(BF16)) * ea_b[:, gs]
        y = y_diag + y_off_f + y_off_b + dskip_ref[:, gs] * xs[:, gs]
        yg = y * zs_ref[:, gs].astype(F32)
        ms = jnp.mean(yg * yg, axis=-1, keepdims=True)
        o_ref[:, gs] = (yg * lax.rsqrt(ms + EPS) * ssdg_ref[:, gs]).astype(BF16)
        bt = b_g.astype(F32).T.astype(BF16)
        state[g] = state[g] * cd_f[:, gs] + _dot(bt, (xs[:, gs] * w_f[:, gs]).astype(BF16))

    kslab = jnp.concatenate([kp_ref[...], kc_ref[...], kn_ref[...]], axis=0).astype(F32)
    lane3 = lax.broadcasted_iota(jnp.int32, (3 * CHUNK, LANES), 1)
    k_g0 = jnp.where(lane3 < ATT_HEAD_DIM, kslab, 0.0).astype(BF16)
    k_g1 = jnp.where(lane3 < ATT_HEAD_DIM, 0.0, kslab).astype(BF16)
    vslab = jnp.concatenate([vp_ref[...], vc_ref[...], vn_ref[...]], axis=0)
    qi = lax.broadcasted_iota(jnp.int32, (CHUNK, 3 * CHUNK), 0)
    kj = lax.broadcasted_iota(jnp.int32, (CHUNK, 3 * CHUNK), 1)
    dist_i = jnp.abs(qi - (kj - CHUNK))
    k_lo = jnp.where(c > 0, 0, CHUNK)
    k_hi = jnp.where(c < n_chunks - 1, 3 * CHUNK, 2 * CHUNK)
    valid = (dist_i <= WINDOW) & (kj >= k_lo) & (kj < k_hi)
    dist = dist_i.astype(F32)
    att0 = SSD_WIDTH
    for s_idx in range(ATT_HEADS // 2):
        ss = slice(s_idx * LANES, (s_idx + 1) * LANES)
        qs = q_ref[:, ss]
        outs = []
        for kz, head in ((k_g0, ATT_HEAD_ORDER[2 * s_idx]), (k_g1, ATT_HEAD_ORDER[2 * s_idx + 1])):
            logits = jnp.where(valid, _dot_nt(qs, kz) - ALIBI_SLOPES[head] * dist, neg_inf)
            sk = sink_ref[head]
            m = jnp.maximum(jnp.max(logits, axis=-1, keepdims=True), sk)
            pr = jnp.exp(logits - m)
            den = jnp.sum(pr, axis=-1, keepdims=True) + jnp.exp(sk - m)
            outs.append(_dot(pr.astype(BF16), vslab) / den)
        o_slab = jnp.where(lo_half, outs[0], outs[1])
        o_ref[:, att0 + s_idx * LANES:att0 + (s_idx + 1) * LANES] = (
            o_slab * za_ref[:, ss].astype(F32)).astype(BF16)

    mem0 = SSD_WIDTH + ATT_WIDTH
    for h in range(MEM_HEADS):
        hs = slice(h * MEM_HEAD_DIM, (h + 1) * MEM_HEAD_DIM)
        s = _dot_nt(mq_ref[:, hs], mk_ref[0, :, hs])
        m = jnp.max(s, axis=-1, keepdims=True)
        pr = jnp.exp(s - m)
        den = jnp.sum(pr, axis=-1, keepdims=True)
        o_h = _dot(pr.astype(BF16), mv_ref[0, :, hs]) / den
        o_ref[:, mem0 + h * MEM_HEAD_DIM:mem0 + (h + 1) * MEM_HEAD_DIM] = (
            o_h * zm_ref[:, hs].astype(F32)).astype(BF16)


def _mixer(xs_c, bc_c, zs, dt, q, k, v, za, mq, zm, sb, mk, mv, p, batch, seq):
    n_chunks = seq // CHUNK
    cur = lambda b, c: (b * n_chunks + c, 0)
    prv = lambda b, c: (b * n_chunks + jnp.maximum(c - 1, 0), 0)
    nxt = lambda b, c: (b * n_chunks + jnp.minimum(c + 1, n_chunks - 1), 0)
    blk = lambda w, im=cur: pl.BlockSpec((CHUNK, w), im)
    state_shape = (SSD_GROUPS, SSD_STATE, GROUP_COLS)
    mem_spec = pl.BlockSpec((1, N_MEM, MEM_WIDTH), lambda b, c: (b, 0, 0))
    return pl.pallas_call(
        functools.partial(_mixer_kernel, n_chunks=n_chunks),
        grid=(batch, n_chunks),
        in_specs=[blk(SSD_WIDTH), blk(BC_WIDTH), blk(SSD_WIDTH), blk(LANES), blk(ATT_WIDTH),
                  blk(KV_WIDTH, prv), blk(KV_WIDTH), blk(KV_WIDTH, nxt),
                  blk(KV_WIDTH, prv), blk(KV_WIDTH), blk(KV_WIDTH, nxt),
                  blk(ATT_WIDTH), blk(MEM_WIDTH), blk(MEM_WIDTH),
                  pl.BlockSpec((1,) + state_shape, lambda b, c: (b * n_chunks + c, 0, 0, 0)),
                  mem_spec, mem_spec,
                  _const_spec((1, LANES)), _const_spec((1, SSD_WIDTH)), _const_spec((1, SSD_WIDTH)),
                  _const_spec((LANES, SSD_WIDTH)), _const_spec((LANES, SSD_WIDTH)),
                  pl.BlockSpec(memory_space=pltpu.SMEM)],
        out_specs=blk(D_MODEL),
        out_shape=jax.ShapeDtypeStruct((batch * seq, D_MODEL), BF16),
        scratch_shapes=[pltpu.VMEM(state_shape, F32)],
        compiler_params=pltpu.CompilerParams(dimension_semantics=("arbitrary", "arbitrary"),
                                             vmem_limit_bytes=VMEM_LIMIT_SMALL),
        name="mixer",
    )(xs_c, bc_c, zs, dt, q, k, k, k, v, v, v, za, mq, zm, sb, mk, mv,
      p["a_log"], p["d_skip"], p["ssd_norm_g"], p["e_fwd"], p["e_bwd"], p["sink"])


def _outproj_kernel(x_ref, o_ref, w_ref, y_ref):
    y_ref[...] = x_ref[...] + _dot(o_ref[...], w_ref[...])


def _outproj(x2d, o2d, w_out):
    rows = x2d.shape[0]
    row_spec = pl.BlockSpec((ROW_TILE, D_MODEL), lambda i: (i, 0))
    return pl.pallas_call(
        _outproj_kernel,
        grid=(rows // ROW_TILE,),
        in_specs=[row_spec, row_spec, _const_spec((D_MODEL, D_MODEL))],
        out_specs=row_spec,
        out_shape=jax.ShapeDtypeStruct((rows, D_MODEL), F32),
        compiler_params=pltpu.CompilerParams(dimension_semantics=("arbitrary",),
                                             vmem_limit_bytes=VMEM_LIMIT_SMALL),
        name="outproj",
    )(x2d, o2d, w_out)


def _block_diag_mean(width, group):
    r = lax.broadcasted_iota(jnp.int32, (width, width), 0) // group
    c = lax.broadcasted_iota(jnp.int32, (width, width), 1) // group
    return jnp.where(r == c, 1.0 / group, 0.0).astype(BF16)


def _expand_matrix(lane_offset):
    r = lax.broadcasted_iota(jnp.int32, (LANES, SSD_WIDTH), 0)
    c = lax.broadcasted_iota(jnp.int32, (LANES, SSD_WIDTH), 1) // SSD_HEAD_DIM
    return jnp.where(r == c + lane_offset, 1.0, 0.0).astype(BF16)


def _pad_lanes(v, width=LANES):
    return jnp.pad(v, ((0, 0), (0, width - v.shape[-1])))


def _pad_rows(v, rows=8):
    return jnp.pad(v, ((0, rows - v.shape[0]), (0, 0)))


def _prepare(norm_g, w_in, conv_w, conv_b, dt_bias, a_log, d_skip, ssd_norm_g, q_norm_g, k_norm_g, sink,
             mem_norm_g, w_mem_kv, mq_norm_g, mk_norm_g, w_out):
    sizes = (SSD_WIDTH + BC_WIDTH, SSD_WIDTH, 2 * SSD_HEADS, ATT_WIDTH, KV_WIDTH, KV_WIDTH, ATT_WIDTH,
             MEM_WIDTH, MEM_WIDTH)
    offs = [0]
    for s in sizes:
        offs.append(offs[-1] + s)
    seg = lambda i: w_in[:, offs[i]:offs[i + 1]]
    head_perm = jnp.concatenate([jnp.arange(h * ATT_HEAD_DIM, (h + 1) * ATT_HEAD_DIM) for h in ATT_HEAD_ORDER])
    w_xbc = seg(0)
    att_rows = w_out[SSD_WIDTH:SSD_WIDTH + ATT_WIDTH][head_perm]
    return {
        "norm_g": norm_g[None, :],
        "w_xs": w_xbc[:, :SSD_WIDTH].astype(BF16),
        "w_bc": w_xbc[:, SSD_WIDTH:].astype(BF16),
        "w_zs": seg(1).astype(BF16),
        "w_dt": _pad_lanes(seg(2)).astype(BF16),
        "w_q": seg(3)[:, head_perm].astype(BF16),
        "w_k": seg(4).astype(BF16),
        "w_v": seg(5).astype(BF16),
        "w_za": seg(6)[:, head_perm].astype(BF16),
        "w_mq": seg(7).astype(BF16),
        "w_zm": seg(8).astype(BF16),
        "gq": jnp.tile(q_norm_g, ATT_HEADS)[None, :],
        "gk": jnp.tile(k_norm_g, ATT_KV_HEADS)[None, :],
        "gmq": jnp.tile(mq_norm_g, MEM_HEADS)[None, :],
        "gmk": jnp.tile(mk_norm_g, MEM_HEADS)[None, :],
        "g64q": _block_diag_mean(ATT_WIDTH, ATT_HEAD_DIM),
        "g64k": _block_diag_mean(KV_WIDTH, ATT_HEAD_DIM),
        "g128": _block_diag_mean(MEM_WIDTH, MEM_HEAD_DIM),
        "dt_bias": _pad_lanes(dt_bias.reshape(1, -1)),
        "a_log": _pad_lanes(a_log.reshape(1, -1)),
        "conv_w_xs": _pad_rows(conv_w[:, :SSD_WIDTH]),
        "conv_w_bc": _pad_rows(conv_w[:, SSD_WIDTH:]),
        "conv_b_xs": conv_b[None, :SSD_WIDTH],
        "conv_b_bc": conv_b[None, SSD_WIDTH:],
        "d_skip": jnp.repeat(d_skip, SSD_HEAD_DIM)[None, :],
        "ssd_norm_g": ssd_norm_g[None, :],
        "sink": sink,
        "mem_norm_g": mem_norm_g[None, :],
        "w_mk": w_mem_kv[:, :MEM_WIDTH].astype(BF16),
        "w_mv": w_mem_kv[:, MEM_WIDTH:].astype(BF16),
        "w_out": jnp.concatenate([w_out[:SSD_WIDTH], att_rows, w_out[SSD_WIDTH + ATT_WIDTH:]], axis=0).astype(BF16),
        "e_fwd": _expand_matrix(0),
        "e_bwd": _expand_matrix(SSD_HEADS),
    }


def _layer(x, mem, p):
    batch, seq, _ = x.shape
    assert seq % CHUNK == 0 and (batch * seq) % ROW_TILE == 0
    x2d = x.reshape(batch * seq, D_MODEL)
    xs_raw, bc_raw, zs, q, k, v, za, mq, zm, dt = _inproj(x2d, p)
    mk, mv = _memkv(mem, p)
    xs_c, bc_c, sb = _conv_state(xs_raw, bc_raw, dt, p, batch, seq)
    o = _mixer(xs_c, bc_c, zs, dt, q, k, v, za, mq, zm, sb, mk, mv, p, batch, seq)
    return _outproj(x2d, o, p["w_out"]).reshape(batch, seq, D_MODEL)


def kernel(x_prompt, x_sample, mem_prompt, mem_sample, norm_g, w_in, conv_w, conv_b, dt_bias, a_log, d_skip,
           ssd_norm_g, q_norm_g, k_norm_g, sink, mem_norm_g, w_mem_kv, mq_norm_g, mk_norm_g, w_out):
    stacked = (norm_g, w_in, conv_w, conv_b, dt_bias, a_log, d_skip, ssd_norm_g, q_norm_g, k_norm_g, sink,
               mem_norm_g, w_mem_kv, mq_norm_g, mk_norm_g, w_out)
    y_prompt, y_sample = x_prompt, x_sample
    for layer in range(norm_g.shape[0]):
        p = _prepare(*[t[layer] for t in stacked])
        y_prompt = _layer(y_prompt, mem_prompt, p)
        y_sample = _layer(y_sample, mem_sample, p)
    return (y_prompt, y_sample)
```

```python
import functools

import jax
import jax.numpy as jnp
from jax import lax
from jax.experimental import pallas as pl
from jax.experimental.pallas import tpu as pltpu

F32 = jnp.float32
BF16 = jnp.bfloat16

D_MODEL = 2048
N_MEM = 256
EPS = 1e-6
SSD_WIDTH = 1024
SSD_HEAD_DIM = 64
SSD_HEADS = 16
SSD_GROUPS = 2
SSD_STATE = 128
CHUNK = 128
D_CONV = 5
ATT_WIDTH = 512
ATT_HEAD_DIM = 64
ATT_HEADS = 8
ATT_KV_HEADS = 2
WINDOW = 128
MEM_WIDTH = 512
MEM_HEADS = 4
MEM_HEAD_DIM = 128
BC_WIDTH = 2 * SSD_GROUPS * SSD_STATE
KV_WIDTH = ATT_KV_HEADS * ATT_HEAD_DIM
K_SLAB_WIDTH = 2 * KV_WIDTH
LOG2E = 1.4426950408889634
GROUP_COLS = SSD_WIDTH // SSD_GROUPS
LANES = 128
HALO = 16
ROW_TILE = 512
ATT_HEAD_ORDER = (0, 4, 1, 5, 2, 6, 3, 7)
ALIBI_SLOPES = tuple(2.0 ** (-8.0 * (h + 1) / ATT_HEADS) for h in range(ATT_HEADS))
VMEM_LIMIT_BIG = 52 * 1024 * 1024
VMEM_LIMIT_SMALL = 40 * 1024 * 1024


def _silu(x):
    return x * jax.nn.sigmoid(x)


def _dot(a, b):
    return jnp.dot(a, b, preferred_element_type=F32)


def _dot_nt(a, b):
    return lax.dot_general(a, b, (((1,), (1,)), ((), ())), preferred_element_type=F32)


def _split2(v):
    hi = v.astype(BF16)
    lo = (v - hi.astype(F32)).astype(BF16)
    return hi, lo


def _split3(v):
    hi = v.astype(BF16)
    r1 = v - hi.astype(F32)
    mid = r1.astype(BF16)
    lo = (r1 - mid.astype(F32)).astype(BF16)
    return hi, mid, lo


def _dot_right_exact(v, m_stacked):
    hi, lo = _split2(v)
    return _dot(jnp.concatenate([hi, lo], axis=1), m_stacked)


def _dot_left_exact(m, v):
    hi, mid, lo = _split3(v)
    wide = _dot(m, jnp.concatenate([hi, mid], axis=1))
    return wide[:, :LANES] + wide[:, LANES:] + _dot(m, lo)


def _head_rms(v, head_dim, heads_per_slab):
    lane = lax.broadcasted_iota(jnp.int32, (v.shape[0], LANES), 1)
    lo = lane < LANES // 2
    outs = []
    for s in range(v.shape[-1] // LANES):
        blk = v[:, s * LANES:(s + 1) * LANES]
        sq = blk * blk
        if heads_per_slab == 1:
            tot = jnp.sum(sq, axis=-1, keepdims=True)
        else:
            tot = jnp.where(lo, jnp.sum(jnp.where(lo, sq, 0.0), axis=-1, keepdims=True),
                            jnp.sum(jnp.where(lo, 0.0, sq), axis=-1, keepdims=True))
        outs.append(blk * lax.rsqrt(tot * (1.0 / head_dim) + EPS))
    return jnp.concatenate(outs, axis=-1)


def _softplus(x):
    return jnp.maximum(x, 0.0) + jnp.log1p(jnp.exp(-jnp.abs(x)))


def _neg_exp_alog(alog_row):
    lane = lax.broadcasted_iota(jnp.int32, alog_row.shape, 1)
    return jnp.where(lane < 2 * SSD_HEADS, -jnp.exp(alog_row), 0.0)


def _inproj_kernel(x_ref, ng_ref, wxs, wbc, wzs, wq, wk, wvdt, wza, wmq, wzm,
                   gq_ref, gk_ref, gmq_ref, dtb_ref,
                   o_xs, o_bc, o_zs, o_q, o_k, o_v, o_za, o_mq, o_zm, o_dt):
    x = x_ref[...]
    ms = jnp.mean(x * x, axis=-1, keepdims=True)
    hn = (x * lax.rsqrt(ms + EPS) * ng_ref[...]).astype(BF16)

    o_xs[...] = _dot(hn, wxs[...]).astype(BF16)
    o_bc[...] = _dot(hn, wbc[...]).astype(BF16)
    o_zs[...] = _silu(_dot(hn, wzs[...])).astype(BF16)
    q = _head_rms(_dot(hn, wq[...]), ATT_HEAD_DIM, 2)
    o_q[...] = (q * gq_ref[...] * (LOG2E * ATT_HEAD_DIM ** -0.5)).astype(BF16)
    k = _head_rms(_dot(hn, wk[...]), ATT_HEAD_DIM, 1)
    o_k[...] = (k * gk_ref[...]).astype(BF16)
    vdt = _dot(hn, wvdt[...])
    o_v[...] = vdt[:, :KV_WIDTH].astype(BF16)
    o_dt[...] = _softplus(vdt[:, KV_WIDTH:] + dtb_ref[...])
    o_za[...] = _silu(_dot(hn, wza[...])).astype(BF16)
    mq = _head_rms(_dot(hn, wmq[...]), MEM_HEAD_DIM, 1)
    o_mq[...] = (mq * gmq_ref[...] * (LOG2E * MEM_HEAD_DIM ** -0.5)).astype(BF16)
    o_zm[...] = _silu(_dot(hn, wzm[...])).astype(BF16)


def _const_spec(shape):
    nd = len(shape)
    return pl.BlockSpec(shape, lambda *_: (0,) * nd, pipeline_mode=pl.Buffered(1))


def _inproj(x2d, p):
    rows = x2d.shape[0]
    assert rows % ROW_TILE == 0
    widths = (SSD_WIDTH, BC_WIDTH, SSD_WIDTH, ATT_WIDTH, K_SLAB_WIDTH, KV_WIDTH, ATT_WIDTH, MEM_WIDTH, MEM_WIDTH)
    weights = (p["w_xs"], p["w_bc"], p["w_zs"], p["w_q"], p["w_k"], p["w_vdt"], p["w_za"], p["w_mq"], p["w_zm"])
    consts = (p["gq"], p["gk"], p["gmq"], p["dt_bias"])
    row_spec = lambda w: pl.BlockSpec((ROW_TILE, w), lambda i: (i, 0))
    out_shape = tuple(jax.ShapeDtypeStruct((rows, w), BF16) for w in widths) + (
        jax.ShapeDtypeStruct((rows, LANES), F32),)
    return pl.pallas_call(
        _inproj_kernel,
        grid=(rows // ROW_TILE,),
        in_specs=[row_spec(D_MODEL), _const_spec((1, D_MODEL))]
        + [_const_spec(w.shape) for w in weights] + [_const_spec(c.shape) for c in consts],
        out_specs=tuple(row_spec(w) for w in widths) + (row_spec(LANES),),
        out_shape=out_shape,
        compiler_params=pltpu.CompilerParams(dimension_semantics=("arbitrary",),
                                             vmem_limit_bytes=VMEM_LIMIT_BIG),
        name="inproj",
    )(x2d, p["norm_g"], *weights, *consts)


def _memkv_kernel(mem_ref, g_ref, wk_ref, wv_ref, gmk_ref, mk_ref, mv_ref):
    m = mem_ref[0]
    ms = jnp.mean(m * m, axis=-1, keepdims=True)
    mn = (m * lax.rsqrt(ms + EPS) * g_ref[...]).astype(BF16)
    mk = _head_rms(_dot(mn, wk_ref[...]), MEM_HEAD_DIM, 1)
    mk_ref[0] = (mk * gmk_ref[...]).astype(BF16)
    mv_ref[0] = _dot(mn, wv_ref[...]).astype(BF16)


def _memkv(mem, p):
    b = mem.shape[0]
    out_spec = pl.BlockSpec((1, N_MEM, MEM_WIDTH), lambda i: (i, 0, 0))
    return pl.pallas_call(
        _memkv_kernel,
        grid=(b,),
        in_specs=[pl.BlockSpec((1, N_MEM, D_MODEL), lambda i: (i, 0, 0)), _const_spec((1, D_MODEL)),
                  _const_spec(p["w_mk"].shape), _const_spec(p["w_mv"].shape),
                  _const_spec((1, MEM_WIDTH))],
        out_specs=(out_spec, out_spec),
        out_shape=(jax.ShapeDtypeStruct((b, N_MEM, MEM_WIDTH), BF16),) * 2,
        compiler_params=pltpu.CompilerParams(dimension_semantics=("arbitrary",),
                                             vmem_limit_bytes=VMEM_LIMIT_SMALL),
        name="memkv",
    )(mem, p["mem_norm_g"], p["w_mk"], p["w_mv"], p["gmk"])


def _conv_state_kernel(xs_ref, xsp_ref, xsn_ref, bc_ref, bcp_ref, bcn_ref, dt_ref, alog_ref,
                       cwx_ref, cbx_ref, cwb_ref, cbb_ref, eb_ref, shift_ref,
                       xso_ref, bco_ref, sb_ref, state, *, n_chunks):
    j = pl.program_id(1)
    c = n_chunks - 1 - j

    @pl.when(j == 0)
    def _():
        state[...] = jnp.zeros_like(state)

    has_prev = c > 0
    has_next = c < n_chunks - 1

    def conv(cur_ref, p_ref, n_ref, w_ref, b_ref):
        prev = jnp.where(has_prev, p_ref[...].astype(F32), 0.0).astype(BF16)
        nxt = jnp.where(has_next, n_ref[...].astype(F32), 0.0).astype(BF16)
        fill = jnp.zeros((CHUNK - 2 * HALO, cur_ref.shape[-1]), BF16)
        ext = jnp.concatenate([prev, cur_ref[...], nxt, fill], axis=0)
        acc = b_ref[...]
        for k in range(D_CONV):
            acc = acc + _dot(shift_ref[k], ext) * w_ref[k:k + 1, :]
        return _silu(acc)

    dt = dt_ref[...]
    a = dt * _neg_exp_alog(alog_ref[...])
    row = lax.broadcasted_iota(jnp.int32, (CHUNK, CHUNK), 0)
    col = lax.broadcasted_iota(jnp.int32, (CHUNK, CHUNK), 1)
    strict_lower = jnp.where(row > col, 1.0, 0.0).astype(BF16)
    pre = _dot_left_exact(strict_lower, a)
    tot = jnp.sum(a, axis=0, keepdims=True)

    xc = conv(xs_ref, xsp_ref, xsn_ref, cwx_ref, cbx_ref)
    xso_ref[...] = xc.astype(BF16)

    scal = jnp.concatenate([dt * jnp.exp(pre), jnp.broadcast_to(jnp.exp(tot), (8, LANES))], axis=0)
    scal_exp = _dot_right_exact(scal, eb_ref[...])
    w_exp = scal_exp[0:CHUNK]
    cd = scal_exp[CHUNK:CHUNK + 1]

    bcc = conv(bc_ref, bcp_ref, bcn_ref, cwb_ref, cbb_ref)
    bco_ref[...] = bcc.astype(BF16)

    xw = (xc * w_exp).astype(BF16)
    for g in range(SSD_GROUPS):
        gs = slice(g * GROUP_COLS, (g + 1) * GROUP_COLS)
        sb_ref[0, g] = state[g]
        bt = bcc[:, g * SSD_STATE:(g + 1) * SSD_STATE].T.astype(BF16)
        state[g] = state[g] * cd[:, gs] + _dot(bt, xw[:, gs])


def _conv_state(xs_raw, bc_raw, dt, p, batch, seq):
    n_chunks = seq // CHUNK
    hb = CHUNK // HALO
    last_hb = seq // HALO - 1
    cur = lambda b, j: (b * n_chunks + (n_chunks - 1 - j), 0)
    prv = lambda b, j: (b * (seq // HALO) + jnp.maximum((n_chunks - 1 - j) * hb - 1, 0), 0)
    nxt = lambda b, j: (b * (seq // HALO) + jnp.minimum((n_chunks - j) * hb, last_hb), 0)
    rows = batch * seq
    state_shape = (SSD_GROUPS, SSD_STATE, GROUP_COLS)
    return pl.pallas_call(
        functools.partial(_conv_state_kernel, n_chunks=n_chunks),
        grid=(batch, n_chunks),
        in_specs=[pl.BlockSpec((CHUNK, SSD_WIDTH), cur), pl.BlockSpec((HALO, SSD_WIDTH), prv),
                  pl.BlockSpec((HALO, SSD_WIDTH), nxt),
                  pl.BlockSpec((CHUNK, BC_WIDTH), cur), pl.BlockSpec((HALO, BC_WIDTH), prv),
                  pl.BlockSpec((HALO, BC_WIDTH), nxt),
                  pl.BlockSpec((CHUNK, LANES), cur), _const_spec((1, LANES)),
                  _const_spec((8, SSD_WIDTH)), _const_spec((1, SSD_WIDTH)),
                  _const_spec((8, BC_WIDTH)), _const_spec((1, BC_WIDTH)),
                  _const_spec((2 * LANES, SSD_WIDTH)), _const_spec((D_CONV, CHUNK, 2 * CHUNK))],
        out_specs=(pl.BlockSpec((CHUNK, SSD_WIDTH), cur), pl.BlockSpec((CHUNK, BC_WIDTH), cur),
                   pl.BlockSpec((1,) + state_shape, lambda b, j: (b * n_chunks + (n_chunks - 1 - j), 0, 0, 0))),
        out_shape=(jax.ShapeDtypeStruct((rows, SSD_WIDTH), BF16), jax.ShapeDtypeStruct((rows, BC_WIDTH), BF16),
                   jax.ShapeDtypeStruct((batch * n_chunks,) + state_shape, F32)),
        scratch_shapes=[pltpu.VMEM(state_shape, F32)],
        compiler_params=pltpu.CompilerParams(dimension_semantics=("arbitrary", "arbitrary"),
                                             vmem_limit_bytes=VMEM_LIMIT_SMALL),
        name="conv_state",
    )(xs_raw, xs_raw, xs_raw, bc_raw, bc_raw, bc_raw, dt, p["a_log"],
      p["conv_w_xs"], p["conv_b_xs"], p["conv_w_bc"], p["conv_b_bc"], p["e_bwd"], p["conv_shift"])


def _mixer_kernel(xs_ref, bc_ref, zs_ref, dt_ref, q_ref, kp_ref, kc_ref, kn_ref, vp_ref, vc_ref, vn_ref,
                  za_ref, mq_ref, zm_ref, sb_ref, mk_ref, mv_ref, bias_ref, alog_ref, dskip_ref, ssdg_ref,
                  ef_ref, eb_ref, sink_ref, o_ref, state):
    c = pl.program_id(1)

    @pl.when(c == 0)
    def _():
        state[...] = jnp.zeros_like(state)

    nh = SSD_HEADS
    n_slabs = ATT_WIDTH // LANES
    mem_cols = [slice(h * MEM_HEAD_DIM, (h + 1) * MEM_HEAD_DIM) for h in range(MEM_HEADS)]
    grp_cols = [slice(g * GROUP_COLS, (g + 1) * GROUP_COLS) for g in range(SSD_GROUPS)]

    dt = dt_ref[...]
    a2 = dt * (_neg_exp_alog(alog_ref[...]) * LOG2E)
    row = lax.broadcasted_iota(jnp.int32, (CHUNK, CHUNK), 0)
    col = lax.broadcasted_iota(jnp.int32, (CHUNK, CHUNK), 1)
    lane = lax.broadcasted_iota(jnp.int32, (CHUNK, LANES), 1)

    q_all = jnp.concatenate([q_ref[:, s * LANES:(s + 1) * LANES] for s in range(n_slabs)], axis=0)
    k_all = jnp.concatenate([r[:, g * LANES:(g + 1) * LANES] for g in range(ATT_KV_HEADS)
                             for r in (kp_ref, kc_ref, kn_ref)], axis=0)
    s_both = _dot_nt(q_all, k_all)
    s_att = [s_both[:, g * 3 * CHUNK:(g + 1) * 3 * CHUNK] for g in range(ATT_KV_HEADS)]
    s_mem = [_dot_nt(mq_ref[:, hs], mk_ref[0, :, hs]) for hs in mem_cols]

    bc_bf = bc_ref[...]
    b_gs = [bc_bf[:, g * SSD_STATE:(g + 1) * SSD_STATE] for g in range(SSD_GROUPS)]
    c_gs = [bc_bf[:, (SSD_GROUPS + g) * SSD_STATE:(SSD_GROUPS + g + 1) * SSD_STATE] for g in range(SSD_GROUPS)]
    cb = [_dot_nt(c_gs[g], b_gs[g]) for g in range(SSD_GROUPS)]
    off_f = [_dot(c_gs[g], state[g].astype(BF16)) for g in range(SSD_GROUPS)]
    off_b = [_dot(c_gs[g], sb_ref[0, g].astype(BF16)) for g in range(SSD_GROUPS)]
    cs_f = _dot_left_exact(jnp.where(row >= col, 1.0, 0.0).astype(BF16), a2)
    cs_b = _dot_left_exact(jnp.where(row <= col, 1.0, 0.0).astype(BF16), a2)

    acs2 = jnp.where(lane < nh, cs_f, cs_b)
    tot2 = jnp.sum(a2, axis=0, keepdims=True)
    r2 = acs2 - jnp.log2(dt)
    acs2_t = acs2.T
    r2_t = r2.T
    ea = jnp.exp2(acs2)
    w_end = dt * jnp.exp2(tot2 - acs2)
    cd8 = jnp.broadcast_to(jnp.exp2(tot2), (8, LANES))
    cbd = [jnp.sum(c_gs[g].astype(F32) * b_gs[g].astype(F32), axis=-1, keepdims=True) for g in range(SSD_GROUPS)]
    u = dt * jnp.where(lane < nh // SSD_GROUPS, cbd[0], cbd[1])
    exp_f = _dot_right_exact(jnp.concatenate([ea, w_end, u, cd8], axis=0), ef_ref[...])
    ea_f = exp_f[0:CHUNK]
    w_f = exp_f[CHUNK:2 * CHUNK]
    u_f = exp_f[2 * CHUNK:3 * CHUNK]
    cd_f = exp_f[3 * CHUNK:3 * CHUNK + 1]
    ea_b = _dot_right_exact(ea, eb_ref[...])

    p_att, inv_att = [], []
    for g in range(ATT_KV_HEADS):
        logits = s_att[g] + bias_ref[0, g]
        ps, invs = [], []
        for s in range(n_slabs):
            blk = logits[s * CHUNK:(s + 1) * CHUNK]
            sk2 = sink_ref[g * n_slabs + s] * LOG2E
            m2 = jnp.maximum(jnp.max(blk, axis=-1, keepdims=True), sk2)
            pr = jnp.exp2(blk - m2)
            den = jnp.sum(pr, axis=-1, keepdims=True) + jnp.exp2(sk2 - m2)
            ps.append(pr.astype(BF16))
            invs.append(1.0 / den)
        p_att.append(jnp.concatenate(ps, axis=0))
        inv_att.append(invs)
    v_cat = jnp.concatenate([vp_ref[...], vc_ref[...], vn_ref[...]], axis=0).astype(F32)
    lane_kv = lax.broadcasted_iota(jnp.int32, v_cat.shape, 1)
    v_all = jnp.concatenate([jnp.where(lane_kv < ATT_HEAD_DIM, v_cat, 0.0),
                             jnp.where(lane_kv < ATT_HEAD_DIM, 0.0, v_cat)], axis=0).astype(BF16)
    p_mem, inv_mem = [], []
    for h in range(MEM_HEADS):
        m2 = jnp.max(s_mem[h], axis=-1, keepdims=True)
        pr = jnp.exp2(s_mem[h] - m2)
        p_mem.append(pr.astype(BF16))
        inv_mem.append(1.0 / jnp.sum(pr, axis=-1, keepdims=True))

    xs = xs_ref[...].astype(F32)
    lo_half = lane < SSD_HEAD_DIM
    strict_lower = row > col
    pairs_per_group = nh // SSD_GROUPS // 2
    m_pairs, rhs_pairs = [], []
    for g in range(SSD_GROUPS):
        for pair in range(pairs_per_group):
            h0 = g * (nh // SSD_GROUPS) + 2 * pair
            ms_ = []
            for h in (h0, h0 + 1):
                arg = jnp.where(strict_lower,
                                acs2[:, h:h + 1] - r2_t[h:h + 1, :],
                                acs2[:, nh + h:nh + h + 1] - r2_t[nh + h:nh + h + 1, :])
                ms_.append((cb[g] * jnp.exp2(arg)).astype(BF16))
            m_pairs.append(jnp.concatenate(ms_, axis=1))
            slab = xs[:, h0 * SSD_HEAD_DIM:(h0 + 2) * SSD_HEAD_DIM]
            rhs_pairs.append(jnp.concatenate([jnp.where(lo_half, slab, 0.0), jnp.where(lo_half, 0.0, slab)],
                                             axis=0).astype(BF16))

    o_att = _dot(jnp.concatenate(p_att, axis=1), v_all)
    o_mem = [_dot(p_mem[h], mv_ref[0, :, mem_cols[h]]) for h in range(MEM_HEADS)]
    y_pairs = [_dot(m, r) for m, r in zip(m_pairs, rhs_pairs)]
    for g in range(SSD_GROUPS):
        gs = grp_cols[g]
        bt = b_gs[g].astype(F32).T.astype(BF16)
        state[g] = state[g] * cd_f[:, gs] + _dot(bt, (xs[:, gs] * w_f[:, gs]).astype(BF16))

    for g in range(SSD_GROUPS):
        gs = grp_cols[g]
        y_diag = jnp.concatenate(y_pairs[g * pairs_per_group:(g + 1) * pairs_per_group], axis=1)
        y = (y_diag + off_f[g] * ea_f[:, gs] + off_b[g] * ea_b[:, gs]
             + (dskip_ref[:, gs] + u_f[:, gs]) * xs[:, gs])
        yg = y * zs_ref[:, gs].astype(F32)
        ms = jnp.mean(yg * yg, axis=-1, keepdims=True)
        o_ref[:, gs] = (yg * lax.rsqrt(ms + EPS) * ssdg_ref[:, gs]).astype(BF16)
    att0 = SSD_WIDTH
    for s in range(n_slabs):
        rows = slice(s * CHUNK, (s + 1) * CHUNK)
        o_slab = o_att[rows] * jnp.where(lo_half, inv_att[0][s], inv_att[1][s])
        o_ref[:, att0 + s * LANES:att0 + (s + 1) * LANES] = (
            o_slab * za_ref[:, s * LANES:(s + 1) * LANES].astype(F32)).astype(BF16)
    mem0 = SSD_WIDTH + ATT_WIDTH
    for h in range(MEM_HEADS):
        hs = mem_cols[h]
        o_ref[:, mem0 + h * MEM_HEAD_DIM:mem0 + (h + 1) * MEM_HEAD_DIM] = (
            o_mem[h] * inv_mem[h] * zm_ref[:, hs].astype(F32)).astype(BF16)


def _mixer(xs_c, bc_c, zs, dt, q, k, v, za, mq, zm, sb, mk, mv, p, batch, seq):
    n_chunks = seq // CHUNK
    assert n_chunks >= 2
    cur = lambda b, c: (b * n_chunks + c, 0)
    prv = lambda b, c: (b * n_chunks + jnp.maximum(c - 1, 0), 0)
    nxt = lambda b, c: (b * n_chunks + jnp.minimum(c + 1, n_chunks - 1), 0)
    blk = lambda w, im=cur: pl.BlockSpec((CHUNK, w), im)
    state_shape = (SSD_GROUPS, SSD_STATE, GROUP_COLS)
    mem_spec = pl.BlockSpec((1, N_MEM, MEM_WIDTH), lambda b, c: (b, 0, 0))
    bias_spec = pl.BlockSpec((1,) + p["att_bias"].shape[1:],
                             lambda b, c: (jnp.minimum(c, 1) + c // (n_chunks - 1), 0, 0, 0))
    return pl.pallas_call(
        _mixer_kernel,
        grid=(batch, n_chunks),
        in_specs=[blk(SSD_WIDTH), blk(BC_WIDTH), blk(SSD_WIDTH), blk(LANES), blk(ATT_WIDTH),
                  blk(K_SLAB_WIDTH, prv), blk(K_SLAB_WIDTH), blk(K_SLAB_WIDTH, nxt),
                  blk(KV_WIDTH, prv), blk(KV_WIDTH), blk(KV_WIDTH, nxt),
                  blk(ATT_WIDTH), blk(MEM_WIDTH), blk(MEM_WIDTH),
                  pl.BlockSpec((1,) + state_shape, lambda b, c: (b * n_chunks + c, 0, 0, 0)),
                  mem_spec, mem_spec, bias_spec,
                  _const_spec((1, LANES)), _const_spec((1, SSD_WIDTH)), _const_spec((1, SSD_WIDTH)),
                  _const_spec((2 * LANES, SSD_WIDTH)), _const_spec((2 * LANES, SSD_WIDTH)),
                  pl.BlockSpec(memory_space=pltpu.SMEM)],
        out_specs=blk(D_MODEL),
        out_shape=jax.ShapeDtypeStruct((batch * seq, D_MODEL), BF16),
        scratch_shapes=[pltpu.VMEM(state_shape, F32)],
        compiler_params=pltpu.CompilerParams(dimension_semantics=("arbitrary", "arbitrary"),
                                             vmem_limit_bytes=VMEM_LIMIT_SMALL),
        name="mixer",
    )(xs_c, bc_c, zs, dt, q, k, k, k, v, v, v, za, mq, zm, sb, mk, mv, p["att_bias"],
      p["a_log"], p["d_skip"], p["ssd_norm_g"], p["e_fwd"], p["e_bwd"], p["sink"])


def _outproj_kernel(x_ref, o_ref, w_ref, y_ref):
    y_ref[...] = x_ref[...] + _dot(o_ref[...], w_ref[...])


def _outproj(x2d, o2d, w_out):
    rows = x2d.shape[0]
    row_spec = pl.BlockSpec((ROW_TILE, D_MODEL), lambda i: (i, 0))
    return pl.pallas_call(
        _outproj_kernel,
        grid=(rows // ROW_TILE,),
        in_specs=[row_spec, row_spec, _const_spec((D_MODEL, D_MODEL))],
        out_specs=row_spec,
        out_shape=jax.ShapeDtypeStruct((rows, D_MODEL), F32),
        compiler_params=pltpu.CompilerParams(dimension_semantics=("arbitrary",),
                                             vmem_limit_bytes=VMEM_LIMIT_SMALL),
        name="outproj",
    )(x2d, o2d, w_out)


def _conv_shift_matrices():
    t = lax.broadcasted_iota(jnp.int32, (D_CONV, CHUNK, 2 * CHUNK), 1)
    src = lax.broadcasted_iota(jnp.int32, (D_CONV, CHUNK, 2 * CHUNK), 2)
    k = lax.broadcasted_iota(jnp.int32, (D_CONV, CHUNK, 2 * CHUNK), 0)
    return jnp.where(src == t + k + (HALO - (D_CONV - 1) // 2), 1.0, 0.0).astype(BF16)


def _expand_matrix(lane_offset):
    r = lax.broadcasted_iota(jnp.int32, (2 * LANES, SSD_WIDTH), 0) % LANES
    c = lax.broadcasted_iota(jnp.int32, (2 * LANES, SSD_WIDTH), 1) // SSD_HEAD_DIM
    return jnp.where(r == c + lane_offset, 1.0, 0.0).astype(BF16)


def _attention_bias_table():
    qi = lax.broadcasted_iota(jnp.int32, (CHUNK, 3 * CHUNK), 0)
    kj = lax.broadcasted_iota(jnp.int32, (CHUNK, 3 * CHUNK), 1)
    dist = jnp.abs(qi - (kj - CHUNK))
    in_band = dist <= WINDOW
    edge_valid = (kj >= CHUNK, kj >= 0, kj < 2 * CHUNK)
    per_kv = ATT_HEADS // ATT_KV_HEADS
    variants = []
    for ok in edge_valid:
        groups = []
        for g in range(ATT_KV_HEADS):
            heads = [jnp.where(in_band & ok, (-LOG2E * ALIBI_SLOPES[g * per_kv + s]) * dist.astype(F32), -jnp.inf)
                     for s in range(per_kv)]
            groups.append(jnp.concatenate(heads, axis=0))
        variants.append(jnp.stack(groups))
    return jnp.stack(variants)


def _pad_lanes(v, width=LANES):
    return jnp.pad(v, ((0, 0), (0, width - v.shape[-1])))


def _pad_rows(v, rows=8):
    return jnp.pad(v, ((0, rows - v.shape[0]), (0, 0)))


def _prepare(norm_g, w_in, conv_w, conv_b, dt_bias, a_log, d_skip, ssd_norm_g, q_norm_g, k_norm_g, sink,
             mem_norm_g, w_mem_kv, mq_norm_g, mk_norm_g, w_out):
    sizes = (SSD_WIDTH + BC_WIDTH, SSD_WIDTH, 2 * SSD_HEADS, ATT_WIDTH, KV_WIDTH, KV_WIDTH, ATT_WIDTH,
             MEM_WIDTH, MEM_WIDTH)
    offs = [0]
    for s in sizes:
        offs.append(offs[-1] + s)
    seg = lambda i: w_in[:, offs[i]:offs[i + 1]]
    head_perm = jnp.concatenate([jnp.arange(h * ATT_HEAD_DIM, (h + 1) * ATT_HEAD_DIM) for h in ATT_HEAD_ORDER])
    w_xbc = seg(0)
    att_rows = w_out[SSD_WIDTH:SSD_WIDTH + ATT_WIDTH][head_perm]
    return {
        "norm_g": norm_g[None, :],
        "w_xs": w_xbc[:, :SSD_WIDTH].astype(BF16),
        "w_bc": w_xbc[:, SSD_WIDTH:].astype(BF16),
        "w_zs": seg(1).astype(BF16),
        "w_vdt": jnp.concatenate([seg(5), _pad_lanes(seg(2))], axis=1).astype(BF16),
        "w_q": seg(3)[:, head_perm].astype(BF16),
        "w_k": jnp.concatenate([seg(4)[:, :ATT_HEAD_DIM], jnp.zeros((D_MODEL, KV_WIDTH), F32),
                                seg(4)[:, ATT_HEAD_DIM:]], axis=1).astype(BF16),
        "w_za": seg(6)[:, head_perm].astype(BF16),
        "w_mq": seg(7).astype(BF16),
        "w_zm": seg(8).astype(BF16),
        "gq": jnp.tile(q_norm_g, ATT_HEADS)[None, :],
        "gk": jnp.tile(k_norm_g, K_SLAB_WIDTH // ATT_HEAD_DIM)[None, :],
        "gmq": jnp.tile(mq_norm_g, MEM_HEADS)[None, :],
        "gmk": jnp.tile(mk_norm_g, MEM_HEADS)[None, :],
        "att_bias": _attention_bias_table(),
        "conv_shift": _conv_shift_matrices(),
        "dt_bias": _pad_lanes(dt_bias.reshape(1, -1)),
        "a_log": _pad_lanes(a_log.reshape(1, -1)),
        "conv_w_xs": _pad_rows(conv_w[:, :SSD_WIDTH]),
        "conv_w_bc": _pad_rows(conv_w[:, SSD_WIDTH:]),
        "conv_b_xs": conv_b[None, :SSD_WIDTH],
        "conv_b_bc": conv_b[None, SSD_WIDTH:],
        "d_skip": jnp.repeat(d_skip, SSD_HEAD_DIM)[None, :],
        "ssd_norm_g": ssd_norm_g[None, :],
        "sink": sink,
        "mem_norm_g": mem_norm_g[None, :],
        "w_mk": w_mem_kv[:, :MEM_WIDTH].astype(BF16),
        "w_mv": w_mem_kv[:, MEM_WIDTH:].astype(BF16),
        "w_out": jnp.concatenate([w_out[:SSD_WIDTH], att_rows, w_out[SSD_WIDTH + ATT_WIDTH:]], axis=0).astype(BF16),
        "e_fwd": _expand_matrix(0),
        "e_bwd": _expand_matrix(SSD_HEADS),
    }


def _layer(x, mem, p):
    batch, seq, _ = x.shape
    assert seq % CHUNK == 0 and (batch * seq) % ROW_TILE == 0
    x2d = x.reshape(batch * seq, D_MODEL)
    xs_raw, bc_raw, zs, q, k, v, za, mq, zm, dt = _inproj(x2d, p)
    mk, mv = _memkv(mem, p)
    xs_c, bc_c, sb = _conv_state(xs_raw, bc_raw, dt, p, batch, seq)
    o = _mixer(xs_c, bc_c, zs, dt, q, k, v, za, mq, zm, sb, mk, mv, p, batch, seq)
    return _outproj(x2d, o, p["w_out"]).reshape(batch, seq, D_MODEL)


def kernel(x_prompt, x_sample, mem_prompt, mem_sample, norm_g, w_in, conv_w, conv_b, dt_bias, a_log, d_skip,
           ssd_norm_g, q_norm_g, k_norm_g, sink, mem_norm_g, w_mem_kv, mq_norm_g, mk_norm_g, w_out):
    stacked = (norm_g, w_in, conv_w, conv_b, dt_bias, a_log, d_skip, ssd_norm_g, q_norm_g, k_norm_g, sink,
               mem_norm_g, w_mem_kv, mq_norm_g, mk_norm_g, w_out)
    y_prompt, y_sample = x_prompt, x_sample
    for layer in range(norm_g.shape[0]):
        p = _prepare(*[t[layer] for t in stacked])
        y_prompt = _layer(y_prompt, mem_prompt, p)
        y_sample = _layer(y_sample, mem_sample, p)
    return (y_prompt, y_sample)
```

```python
import functools

import jax
import jax.numpy as jnp
from jax import lax
from jax.experimental import pallas as pl
from jax.experimental.pallas import tpu as pltpu

F32 = jnp.float32
BF16 = jnp.bfloat16

D_MODEL = 2048
N_MEM = 256
EPS = 1e-6
SSD_WIDTH = 1024
SSD_HEAD_DIM = 64
SSD_HEADS = 16
SSD_GROUPS = 2
SSD_STATE = 128
CHUNK = 128
D_CONV = 5
ATT_WIDTH = 512
ATT_HEAD_DIM = 64
ATT_HEADS = 8
ATT_KV_HEADS = 2
WINDOW = 128
MEM_WIDTH = 512
MEM_HEADS = 4
MEM_HEAD_DIM = 128
BC_WIDTH = 2 * SSD_GROUPS * SSD_STATE
KV_WIDTH = ATT_KV_HEADS * ATT_HEAD_DIM
K_SLAB_WIDTH = 2 * KV_WIDTH
LOG2E = 1.4426950408889634
GROUP_COLS = SSD_WIDTH // SSD_GROUPS
LANES = 128
HALO = 16
ROW_TILE = 512
REV_CHUNKS = 4
MIX_CHUNKS = 2
CONV_COLS = 256
ATT_HEAD_ORDER = (0, 4, 1, 5, 2, 6, 3, 7)
ALIBI_SLOPES = tuple(2.0 ** (-8.0 * (h + 1) / ATT_HEADS) for h in range(ATT_HEADS))
VMEM_LIMIT_BIG = 52 * 1024 * 1024
VMEM_LIMIT_SMALL = 40 * 1024 * 1024


def _silu(x):
    return x * jax.nn.sigmoid(x)


def _dot(a, b):
    return jnp.dot(a, b, preferred_element_type=F32)


def _dot_nt(a, b):
    return lax.dot_general(a, b, (((1,), (1,)), ((), ())), preferred_element_type=F32)


def _split2(v):
    hi = v.astype(BF16)
    lo = (v - hi.astype(F32)).astype(BF16)
    return hi, lo


def _split3(v):
    hi = v.astype(BF16)
    r1 = v - hi.astype(F32)
    mid = r1.astype(BF16)
    lo = (r1 - mid.astype(F32)).astype(BF16)
    return hi, mid, lo


def _dot_right_exact(v, m_stacked):
    hi, lo = _split2(v)
    return _dot(jnp.concatenate([hi, lo], axis=1), m_stacked)


def _dot_left_exact(m, v):
    hi, mid, lo = _split3(v)
    wide = _dot(m, jnp.concatenate([hi, mid], axis=1))
    return wide[:, :LANES] + wide[:, LANES:] + _dot(m, lo)


def _head_rms(v, head_dim, heads_per_slab):
    lane = lax.broadcasted_iota(jnp.int32, (v.shape[0], LANES), 1)
    lo = lane < LANES // 2
    outs = []
    for s in range(v.shape[-1] // LANES):
        blk = v[:, s * LANES:(s + 1) * LANES]
        sq = blk * blk
        if heads_per_slab == 1:
            tot = jnp.sum(sq, axis=-1, keepdims=True)
        else:
            tot = jnp.where(lo, jnp.sum(jnp.where(lo, sq, 0.0), axis=-1, keepdims=True),
                            jnp.sum(jnp.where(lo, 0.0, sq), axis=-1, keepdims=True))
        outs.append(blk * lax.rsqrt(tot * (1.0 / head_dim) + EPS))
    return jnp.concatenate(outs, axis=-1)


def _softplus(x):
    return jnp.maximum(x, 0.0) + jnp.log1p(jnp.exp(-jnp.abs(x)))


def _neg_exp_alog(alog_row):
    lane = lax.broadcasted_iota(jnp.int32, alog_row.shape, 1)
    return jnp.where(lane < 2 * SSD_HEADS, -jnp.exp(alog_row), 0.0)


def _inproj_kernel(x_ref, xp_ref, xn_ref, ng_ref, wxs, wbc, wzs, wq, wk, wvdt, wza, wmq, wzm,
                   gq_ref, gk_ref, gmq_ref, dtb_ref, cwx_ref, cbx_ref, cwb_ref, cbb_ref,
                   o_xs, o_bc, o_zs, o_q, o_k, o_v, o_za, o_mq, o_zm, o_dt, ext_x, ext_b, *, tiles_per_seq):
    pos = lax.rem(pl.program_id(0), tiles_per_seq)
    has_prev = pos != 0
    has_next = pos != tiles_per_seq - 1

    def normed(x):
        ms = jnp.mean(x * x, axis=-1, keepdims=True)
        return x * lax.rsqrt(ms + EPS) * ng_ref[...]

    hn = normed(x_ref[...]).astype(BF16)
    h_prev = jnp.where(has_prev, normed(xp_ref[...]), 0.0).astype(BF16)
    h_next = jnp.where(has_next, normed(xn_ref[...]), 0.0).astype(BF16)
    hn_ext = jnp.concatenate([h_prev, hn, h_next], axis=0)

    def conv_block(w_ref, ext, cw_ref, cb_ref, o_ref, blk):
        cs = slice(blk * CONV_COLS, (blk + 1) * CONV_COLS)
        proj = _dot(hn_ext, w_ref[:, cs])
        n_rows = proj.shape[0]
        acc = cb_ref[:, cs]
        for k in range(D_CONV):
            shifted = pltpu.roll(proj, ((D_CONV - 1) // 2 - k) % n_rows, 0) if k != (D_CONV - 1) // 2 else proj
            acc = acc + shifted[HALO:HALO + ROW_TILE] * cw_ref[k:k + 1, cs]
        o_ref[:, cs] = _silu(acc).astype(BF16)

    def gate(w_ref, o_ref):
        o_ref[...] = _silu(_dot(hn, w_ref[...])).astype(BF16)

    xs_blocks = [functools.partial(conv_block, wxs, ext_x, cwx_ref, cbx_ref, o_xs, b)
                 for b in range(SSD_WIDTH // CONV_COLS)]
    bc_blocks = [functools.partial(conv_block, wbc, ext_b, cwb_ref, cbb_ref, o_bc, b)
                 for b in range(BC_WIDTH // CONV_COLS)]
    xs_blocks[0]()
    gate(wzs, o_zs)
    xs_blocks[1]()
    q = _head_rms(_dot(hn, wq[...]), ATT_HEAD_DIM, 2)
    o_q[...] = (q * gq_ref[...] * (LOG2E * ATT_HEAD_DIM ** -0.5)).astype(BF16)
    xs_blocks[2]()
    k = _head_rms(_dot(hn, wk[...]), ATT_HEAD_DIM, 1)
    o_k[...] = (k * gk_ref[...]).astype(BF16)
    vdt = _dot(hn, wvdt[...])
    o_v[...] = vdt[:, :KV_WIDTH].astype(BF16)
    o_dt[...] = _softplus(vdt[:, KV_WIDTH:] + dtb_ref[...])
    xs_blocks[3]()
    gate(wza, o_za)
    bc_blocks[0]()
    mq = _head_rms(_dot(hn, wmq[...]), MEM_HEAD_DIM, 1)
    o_mq[...] = (mq * gmq_ref[...] * (LOG2E * MEM_HEAD_DIM ** -0.5)).astype(BF16)
    bc_blocks[1]()
    gate(wzm, o_zm)


def _const_spec(shape):
    nd = len(shape)
    return pl.BlockSpec(shape, lambda *_: (0,) * nd, pipeline_mode=pl.Buffered(1))


def _inproj(x2d, p, seq):
    rows = x2d.shape[0]
    assert rows % ROW_TILE == 0 and seq % ROW_TILE == 0
    widths = (SSD_WIDTH, BC_WIDTH, SSD_WIDTH, ATT_WIDTH, K_SLAB_WIDTH, KV_WIDTH, ATT_WIDTH, MEM_WIDTH, MEM_WIDTH)
    weights = (p["w_xs"], p["w_bc"], p["w_zs"], p["w_q"], p["w_k"], p["w_vdt"], p["w_za"], p["w_mq"], p["w_zm"])
    consts = (p["gq"], p["gk"], p["gmq"], p["dt_bias"],
              p["conv_w_xs"], p["conv_b_xs"], p["conv_w_bc"], p["conv_b_bc"])
    row_spec = lambda w: pl.BlockSpec((ROW_TILE, w), lambda i: (i, 0))
    hb = ROW_TILE // HALO
    last_hb = rows // HALO - 1
    prev_spec = pl.BlockSpec((HALO, D_MODEL), lambda i: (jnp.maximum(i * hb - 1, 0), 0))
    next_spec = pl.BlockSpec((HALO, D_MODEL), lambda i: (jnp.minimum((i + 1) * hb, last_hb), 0))
    out_shape = tuple(jax.ShapeDtypeStruct((rows, w), BF16) for w in widths) + (
        jax.ShapeDtypeStruct((rows, LANES), F32),)
    ext_rows = ROW_TILE + 2 * HALO
    return pl.pallas_call(
        functools.partial(_inproj_kernel, tiles_per_seq=seq // ROW_TILE),
        grid=(rows // ROW_TILE,),
        in_specs=[row_spec(D_MODEL), prev_spec, next_spec, _const_spec((1, D_MODEL))]
        + [_const_spec(w.shape) for w in weights] + [_const_spec(c.shape) for c in consts],
        out_specs=tuple(row_spec(w) for w in widths) + (row_spec(LANES),),
        out_shape=out_shape,
        scratch_shapes=[pltpu.VMEM((ext_rows, SSD_WIDTH), F32), pltpu.VMEM((ext_rows, BC_WIDTH), F32)],
        compiler_params=pltpu.CompilerParams(dimension_semantics=("arbitrary",),
                                             vmem_limit_bytes=VMEM_LIMIT_BIG),
        name="inproj",
    )(x2d, x2d, x2d, p["norm_g"], *weights, *consts)


def _memkv_kernel(mem_ref, g_ref, wk_ref, wv_ref, gmk_ref, mk_ref, mv_ref):
    m = mem_ref[0]
    ms = jnp.mean(m * m, axis=-1, keepdims=True)
    mn = (m * lax.rsqrt(ms + EPS) * g_ref[...]).astype(BF16)
    mk = _head_rms(_dot(mn, wk_ref[...]), MEM_HEAD_DIM, 1)
    mk_ref[0] = (mk * gmk_ref[...]).astype(BF16)
    mv_ref[0] = _dot(mn, wv_ref[...]).astype(BF16)


def _memkv(mem, p):
    b = mem.shape[0]
    out_spec = pl.BlockSpec((1, N_MEM, MEM_WIDTH), lambda i: (i, 0, 0))
    return pl.pallas_call(
        _memkv_kernel,
        grid=(b,),
        in_specs=[pl.BlockSpec((1, N_MEM, D_MODEL), lambda i: (i, 0, 0)), _const_spec((1, D_MODEL)),
                  _const_spec(p["w_mk"].shape), _const_spec(p["w_mv"].shape),
                  _const_spec((1, MEM_WIDTH))],
        out_specs=(out_spec, out_spec),
        out_shape=(jax.ShapeDtypeStruct((b, N_MEM, MEM_WIDTH), BF16),) * 2,
        compiler_params=pltpu.CompilerParams(dimension_semantics=("arbitrary",),
                                             vmem_limit_bytes=VMEM_LIMIT_SMALL),
        name="memkv",
    )(mem, p["mem_norm_g"], p["w_mk"], p["w_mv"], p["gmk"])


def _rev_state_kernel(xs_ref, b_ref, dt_ref, alog_ref, eb_ref, sb_ref, state):
    @pl.when(pl.program_id(1) == 0)
    def _():
        state[...] = jnp.zeros_like(state)

    neg_a = _neg_exp_alog(alog_ref[...])
    row = lax.broadcasted_iota(jnp.int32, (CHUNK, CHUNK), 0)
    col = lax.broadcasted_iota(jnp.int32, (CHUNK, CHUNK), 1)
    strict_lower = jnp.where(row > col, 1.0, 0.0).astype(BF16)
    subs = [slice(s * CHUNK, (s + 1) * CHUNK) for s in range(REV_CHUNKS)]
    weights, decays = [], []
    for rows in subs:
        dt = dt_ref[rows, :]
        a = dt * neg_a
        pre = _dot_left_exact(strict_lower, a)
        weights.append(dt * jnp.exp(pre))
        decays.append(jnp.broadcast_to(jnp.exp(jnp.sum(a, axis=0, keepdims=True)), (8, LANES)))
    scal_exp = _dot_right_exact(jnp.concatenate(weights + decays, axis=0), eb_ref[...])
    for s in reversed(range(REV_CHUNKS)):
        rows = subs[s]
        w_exp = scal_exp[rows]
        cd = scal_exp[REV_CHUNKS * CHUNK + 8 * s:REV_CHUNKS * CHUNK + 8 * s + 1]
        xw = (xs_ref[rows, :].astype(F32) * w_exp).astype(BF16)
        for g in range(SSD_GROUPS):
            gs = slice(g * GROUP_COLS, (g + 1) * GROUP_COLS)
            sb_ref[s, g] = state[g]
            bt = b_ref[rows, g * SSD_STATE:(g + 1) * SSD_STATE].astype(F32).T.astype(BF16)
            state[g] = state[g] * cd[:, gs] + _dot(bt, xw[:, gs])


def _rev_state(xs_c, bc_c, dt, p, batch, seq):
    n_chunks = seq // CHUNK
    assert n_chunks % REV_CHUNKS == 0
    n_blocks = n_chunks // REV_CHUNKS
    blk_rows = REV_CHUNKS * CHUNK
    cur = lambda b, j: (b * n_blocks + (n_blocks - 1 - j), 0)
    state_shape = (SSD_GROUPS, SSD_STATE, GROUP_COLS)
    return pl.pallas_call(
        _rev_state_kernel,
        grid=(batch, n_blocks),
        in_specs=[pl.BlockSpec((blk_rows, SSD_WIDTH), cur),
                  pl.BlockSpec((blk_rows, SSD_GROUPS * SSD_STATE), cur),
                  pl.BlockSpec((blk_rows, LANES), cur), _const_spec((1, LANES)),
                  _const_spec((2 * LANES, SSD_WIDTH))],
        out_specs=pl.BlockSpec((REV_CHUNKS,) + state_shape,
                               lambda b, j: (b * n_blocks + (n_blocks - 1 - j), 0, 0, 0)),
        out_shape=jax.ShapeDtypeStruct((batch * n_chunks,) + state_shape, F32),
        scratch_shapes=[pltpu.VMEM(state_shape, F32)],
        compiler_params=pltpu.CompilerParams(dimension_semantics=("arbitrary", "arbitrary"),
                                             vmem_limit_bytes=VMEM_LIMIT_SMALL),
        name="rev_state",
    )(xs_c, bc_c, dt, p["a_log"], p["e_bwd"])


def _mixer_kernel(xs_ref, bc_ref, zs_ref, dt_ref, q_ref, kp_ref, kc_ref, kn_ref, vp_ref, vc_ref, vn_ref,
                  za_ref, mq_ref, zm_ref, sb_ref, mk_ref, mv_ref, bias_lo_ref, bias_hi_ref, alog_ref, dskip_ref,
                  ssdg_ref, ef_ref, eb_ref, sink_ref, o_ref, state):
    @pl.when(pl.program_id(1) == 0)
    def _():
        state[...] = jnp.zeros_like(state)

    nh = SSD_HEADS
    n_slabs = ATT_WIDTH // LANES
    pairs_per_group = nh // SSD_GROUPS // 2
    mem_cols = [slice(h * MEM_HEAD_DIM, (h + 1) * MEM_HEAD_DIM) for h in range(MEM_HEADS)]
    grp_cols = [slice(g * GROUP_COLS, (g + 1) * GROUP_COLS) for g in range(SSD_GROUPS)]
    row = lax.broadcasted_iota(jnp.int32, (CHUNK, CHUNK), 0)
    col = lax.broadcasted_iota(jnp.int32, (CHUNK, CHUNK), 1)
    lane = lax.broadcasted_iota(jnp.int32, (CHUNK, LANES), 1)
    lo_half = lane < SSD_HEAD_DIM
    strict_lower = row > col
    incl_lower = jnp.where(row >= col, 1.0, 0.0).astype(BF16)
    incl_upper = jnp.where(row <= col, 1.0, 0.0).astype(BF16)
    neg_a2 = _neg_exp_alog(alog_ref[...]) * LOG2E

    def kv_blocks(prev_ref, cur_ref, next_ref, sub):
        cur = [cur_ref[s * CHUNK:(s + 1) * CHUNK, :] for s in range(MIX_CHUNKS)]
        seq = [prev_ref[...]] + cur + [next_ref[...]]
        return seq[sub:sub + 3]

    def stage1(sub):
        rows = slice(sub * CHUNK, (sub + 1) * CHUNK)
        d = {"rows": rows}
        d["dt"] = dt_ref[rows, :]
        d["a2"] = d["dt"] * neg_a2
        q_all = jnp.concatenate([q_ref[rows, s * LANES:(s + 1) * LANES] for s in range(n_slabs)], axis=0)
        k3 = kv_blocks(kp_ref, kc_ref, kn_ref, sub)
        k_all = jnp.concatenate([blk[:, g * LANES:(g + 1) * LANES] for g in range(ATT_KV_HEADS) for blk in k3],
                                axis=0)
        s_both = _dot_nt(q_all, k_all)
        d["s_att"] = [s_both[:, g * 3 * CHUNK:(g + 1) * 3 * CHUNK] for g in range(ATT_KV_HEADS)]
        d["s_mem"] = [_dot_nt(mq_ref[rows, hs], mk_ref[0, :, hs]) for hs in mem_cols]
        bc_bf = bc_ref[rows, :]
        d["b_gs"] = [bc_bf[:, g * SSD_STATE:(g + 1) * SSD_STATE] for g in range(SSD_GROUPS)]
        d["c_gs"] = [bc_bf[:, (SSD_GROUPS + g) * SSD_STATE:(SSD_GROUPS + g + 1) * SSD_STATE]
                     for g in range(SSD_GROUPS)]
        d["cb"] = [_dot_nt(d["c_gs"][g], d["b_gs"][g]) for g in range(SSD_GROUPS)]
        d["off_b"] = [_dot(d["c_gs"][g], sb_ref[sub, g].astype(BF16)) for g in range(SSD_GROUPS)]
        d["cs_f"] = _dot_left_exact(incl_lower, d["a2"])
        d["cs_b"] = _dot_left_exact(incl_upper, d["a2"])
        return d

    def carried_state_term(d):
        d["off_f"] = [_dot(d["c_gs"][g], state[g].astype(BF16)) for g in range(SSD_GROUPS)]

    def stage2(sub, d):
        rows, dt, a2 = d["rows"], d["dt"], d["a2"]
        acs2 = jnp.where(lane < nh, d["cs_f"], d["cs_b"])
        tot2 = jnp.sum(a2, axis=0, keepdims=True)
        r2 = acs2 - jnp.log2(dt)
        acs2_t = acs2.T
        r2_t = r2.T
        ea = jnp.exp2(acs2)
        w_end = dt * jnp.exp2(tot2 - acs2)
        cd8 = jnp.broadcast_to(jnp.exp2(tot2), (8, LANES))
        cbd = [jnp.sum(d["c_gs"][g].astype(F32) * d["b_gs"][g].astype(F32), axis=-1, keepdims=True)
               for g in range(SSD_GROUPS)]
        u = dt * jnp.where(lane < nh // SSD_GROUPS, cbd[0], cbd[1])
        exp_f = _dot_right_exact(jnp.concatenate([ea, w_end, u, cd8], axis=0), ef_ref[...])
        d["ea_f"] = exp_f[0:CHUNK]
        d["w_f"] = exp_f[CHUNK:2 * CHUNK]
        d["u_f"] = exp_f[2 * CHUNK:3 * CHUNK]
        d["cd_f"] = exp_f[3 * CHUNK:3 * CHUNK + 1]
        d["ea_b"] = _dot_right_exact(ea, eb_ref[...])

        bias_ref = bias_lo_ref if sub == 0 else bias_hi_ref
        p_att, inv_att = [], []
        for g in range(ATT_KV_HEADS):
            logits = d["s_att"][g] + bias_ref[0, g]
            ps, invs = [], []
            for s in range(n_slabs):
                blk = logits[s * CHUNK:(s + 1) * CHUNK]
                sk2 = sink_ref[g * n_slabs + s] * LOG2E
                m2 = jnp.maximum(jnp.max(blk, axis=-1, keepdims=True), sk2)
                pr = jnp.exp2(blk - m2)
                den = jnp.sum(pr, axis=-1, keepdims=True) + jnp.exp2(sk2 - m2)
                ps.append(pr.astype(BF16))
                invs.append(1.0 / den)
            p_att.append(jnp.concatenate(ps, axis=0))
            inv_att.append(invs)
        d["p_att"], d["inv_att"] = p_att, inv_att
        v_cat = jnp.concatenate(kv_blocks(vp_ref, vc_ref, vn_ref, sub), axis=0).astype(F32)
        lane_kv = lax.broadcasted_iota(jnp.int32, v_cat.shape, 1)
        d["v_all"] = jnp.concatenate([jnp.where(lane_kv < ATT_HEAD_DIM, v_cat, 0.0),
                                      jnp.where(lane_kv < ATT_HEAD_DIM, 0.0, v_cat)], axis=0).astype(BF16)
        p_mem, inv_mem = [], []
        for h in range(MEM_HEADS):
            m2 = jnp.max(d["s_mem"][h], axis=-1, keepdims=True)
            pr = jnp.exp2(d["s_mem"][h] - m2)
            p_mem.append(pr.astype(BF16))
            inv_mem.append(1.0 / jnp.sum(pr, axis=-1, keepdims=True))
        d["p_mem"], d["inv_mem"] = p_mem, inv_mem

        xs = xs_ref[rows, :].astype(F32)
        d["xs"] = xs
        m_pairs, rhs_pairs = [], []
        for g in range(SSD_GROUPS):
            for pair in range(pairs_per_group):
                h0 = g * (nh // SSD_GROUPS) + 2 * pair
                ms_ = []
                for h in (h0, h0 + 1):
                    arg = jnp.where(strict_lower,
                                    acs2[:, h:h + 1] - r2_t[h:h + 1, :],
                                    acs2[:, nh + h:nh + h + 1] - r2_t[nh + h:nh + h + 1, :])
                    ms_.append((d["cb"][g] * jnp.exp2(arg)).astype(BF16))
                m_pairs.append(jnp.concatenate(ms_, axis=1))
                slab = xs[:, h0 * SSD_HEAD_DIM:(h0 + 2) * SSD_HEAD_DIM]
                rhs_pairs.append(jnp.concatenate([jnp.where(lo_half, slab, 0.0), jnp.where(lo_half, 0.0, slab)],
                                                 axis=0).astype(BF16))
        d["m_pairs"], d["rhs_pairs"] = m_pairs, rhs_pairs

    def stage3(d):
        rows, xs = d["rows"], d["xs"]
        o_att = _dot(jnp.concatenate(d["p_att"], axis=1), d["v_all"])
        o_mem = [_dot(d["p_mem"][h], mv_ref[0, :, mem_cols[h]]) for h in range(MEM_HEADS)]
        y_pairs = [_dot(m, r) for m, r in zip(d["m_pairs"], d["rhs_pairs"])]
        for g in range(SSD_GROUPS):
            gs = grp_cols[g]
            bt = d["b_gs"][g].astype(F32).T.astype(BF16)
            state[g] = state[g] * d["cd_f"][:, gs] + _dot(bt, (xs[:, gs] * d["w_f"][:, gs]).astype(BF16))

        for g in range(SSD_GROUPS):
            gs = grp_cols[g]
            y_diag = jnp.concatenate(y_pairs[g * pairs_per_group:(g + 1) * pairs_per_group], axis=1)
            y = (y_diag + d["off_f"][g] * d["ea_f"][:, gs] + d["off_b"][g] * d["ea_b"][:, gs]
                 + (dskip_ref[:, gs] + d["u_f"][:, gs]) * xs[:, gs])
            yg = y * zs_ref[rows, gs].astype(F32)
            ms = jnp.mean(yg * yg, axis=-1, keepdims=True)
            o_ref[rows, gs] = (yg * lax.rsqrt(ms + EPS) * ssdg_ref[:, gs]).astype(BF16)
        att0 = SSD_WIDTH
        for s in range(n_slabs):
            srows = slice(s * CHUNK, (s + 1) * CHUNK)
            o_slab = o_att[srows] * jnp.where(lo_half, d["inv_att"][0][s], d["inv_att"][1][s])
            o_ref[rows, att0 + s * LANES:att0 + (s + 1) * LANES] = (
                o_slab * za_ref[rows, s * LANES:(s + 1) * LANES].astype(F32)).astype(BF16)
        mem0 = SSD_WIDTH + ATT_WIDTH
        for h in range(MEM_HEADS):
            hs = mem_cols[h]
            o_ref[rows, mem0 + h * MEM_HEAD_DIM:mem0 + (h + 1) * MEM_HEAD_DIM] = (
                o_mem[h] * d["inv_mem"][h] * zm_ref[rows, hs].astype(F32)).astype(BF16)

    chunks = [stage1(sub) for sub in range(MIX_CHUNKS)]
    carried_state_term(chunks[0])
    for sub in range(MIX_CHUNKS):
        stage2(sub, chunks[sub])
    for sub in range(MIX_CHUNKS):
        if sub > 0:
            carried_state_term(chunks[sub])
        stage3(chunks[sub])


def _mixer(xs_c, bc_c, zs, dt, q, k, v, za, mq, zm, sb, mk, mv, p, batch, seq):
    n_chunks = seq // CHUNK
    assert MIX_CHUNKS == 2 and n_chunks % MIX_CHUNKS == 0
    n_blocks = n_chunks // MIX_CHUNKS
    cur = lambda b, i: (b * n_blocks + i, 0)
    prv = lambda b, i: (b * n_chunks + jnp.maximum(i * MIX_CHUNKS - 1, 0), 0)
    nxt = lambda b, i: (b * n_chunks + jnp.minimum((i + 1) * MIX_CHUNKS, n_chunks - 1), 0)
    blk = lambda w: pl.BlockSpec((MIX_CHUNKS * CHUNK, w), cur)
    halo = lambda w, im: pl.BlockSpec((CHUNK, w), im)
    state_shape = (SSD_GROUPS, SSD_STATE, GROUP_COLS)
    mem_spec = pl.BlockSpec((1, N_MEM, MEM_WIDTH), lambda b, i: (b, 0, 0))
    bias_block = (1,) + p["att_bias"].shape[1:]
    bias_lo = pl.BlockSpec(bias_block, lambda b, i: (jnp.minimum(i, 1), 0, 0, 0))
    bias_hi = pl.BlockSpec(bias_block, lambda b, i: (1 + (i + 1) // n_blocks, 0, 0, 0))
    return pl.pallas_call(
        _mixer_kernel,
        grid=(batch, n_blocks),
        in_specs=[blk(SSD_WIDTH), blk(BC_WIDTH), blk(SSD_WIDTH), blk(LANES), blk(ATT_WIDTH),
                  halo(K_SLAB_WIDTH, prv), blk(K_SLAB_WIDTH), halo(K_SLAB_WIDTH, nxt),
                  halo(KV_WIDTH, prv), blk(KV_WIDTH), halo(KV_WIDTH, nxt),
                  blk(ATT_WIDTH), blk(MEM_WIDTH), blk(MEM_WIDTH),
                  pl.BlockSpec((MIX_CHUNKS,) + state_shape, lambda b, i: (b * n_blocks + i, 0, 0, 0)),
                  mem_spec, mem_spec, bias_lo, bias_hi,
                  _const_spec((1, LANES)), _const_spec((1, SSD_WIDTH)), _const_spec((1, SSD_WIDTH)),
                  _const_spec((2 * LANES, SSD_WIDTH)), _const_spec((2 * LANES, SSD_WIDTH)),
                  pl.BlockSpec(memory_space=pltpu.SMEM)],
        out_specs=blk(D_MODEL),
        out_shape=jax.ShapeDtypeStruct((batch * seq, D_MODEL), BF16),
        scratch_shapes=[pltpu.VMEM(state_shape, F32)],
        compiler_params=pltpu.CompilerParams(dimension_semantics=("arbitrary", "arbitrary"),
                                             vmem_limit_bytes=VMEM_LIMIT_SMALL),
        name="mixer",
    )(xs_c, bc_c, zs, dt, q, k, k, k, v, v, v, za, mq, zm, sb, mk, mv, p["att_bias"], p["att_bias"],
      p["a_log"], p["d_skip"], p["ssd_norm_g"], p["e_fwd"], p["e_bwd"], p["sink"])


def _outproj_kernel(x_ref, o_ref, w_ref, y_ref):
    y_ref[...] = x_ref[...] + _dot(o_ref[...], w_ref[...])


def _outproj(x2d, o2d, w_out):
    rows = x2d.shape[0]
    row_spec = pl.BlockSpec((ROW_TILE, D_MODEL), lambda i: (i, 0))
    return pl.pallas_call(
        _outproj_kernel,
        grid=(rows // ROW_TILE,),
        in_specs=[row_spec, row_spec, _const_spec((D_MODEL, D_MODEL))],
        out_specs=row_spec,
        out_shape=jax.ShapeDtypeStruct((rows, D_MODEL), F32),
        compiler_params=pltpu.CompilerParams(dimension_semantics=("arbitrary",),
                                             vmem_limit_bytes=VMEM_LIMIT_SMALL),
        name="outproj",
    )(x2d, o2d, w_out)


def _expand_matrix(lane_offset):
    r = lax.broadcasted_iota(jnp.int32, (2 * LANES, SSD_WIDTH), 0) % LANES
    c = lax.broadcasted_iota(jnp.int32, (2 * LANES, SSD_WIDTH), 1) // SSD_HEAD_DIM
    return jnp.where(r == c + lane_offset, 1.0, 0.0).astype(BF16)


def _attention_bias_table():
    qi = lax.broadcasted_iota(jnp.int32, (CHUNK, 3 * CHUNK), 0)
    kj = lax.broadcasted_iota(jnp.int32, (CHUNK, 3 * CHUNK), 1)
    dist = jnp.abs(qi - (kj - CHUNK))
    in_band = dist <= WINDOW
    edge_valid = (kj >= CHUNK, kj >= 0, kj < 2 * CHUNK)
    per_kv = ATT_HEADS // ATT_KV_HEADS
    variants = []
    for ok in edge_valid:
        groups = []
        for g in range(ATT_KV_HEADS):
            heads = [jnp.where(in_band & ok, (-LOG2E * ALIBI_SLOPES[g * per_kv + s]) * dist.astype(F32), -jnp.inf)
                     for s in range(per_kv)]
            groups.append(jnp.concatenate(heads, axis=0))
        variants.append(jnp.stack(groups))
    return jnp.stack(variants)


def _pad_lanes(v, width=LANES):
    return jnp.pad(v, ((0, 0), (0, width - v.shape[-1])))


def _pad_rows(v, rows=8):
    return jnp.pad(v, ((0, rows - v.shape[0]), (0, 0)))


def _prepare(norm_g, w_in, conv_w, conv_b, dt_bias, a_log, d_skip, ssd_norm_g, q_norm_g, k_norm_g, sink,
             mem_norm_g, w_mem_kv, mq_norm_g, mk_norm_g, w_out):
    sizes = (SSD_WIDTH + BC_WIDTH, SSD_WIDTH, 2 * SSD_HEADS, ATT_WIDTH, KV_WIDTH, KV_WIDTH, ATT_WIDTH,
             MEM_WIDTH, MEM_WIDTH)
    offs = [0]
    for s in sizes:
        offs.append(offs[-1] + s)
    seg = lambda i: w_in[:, offs[i]:offs[i + 1]]
    head_perm = jnp.concatenate([jnp.arange(h * ATT_HEAD_DIM, (h + 1) * ATT_HEAD_DIM) for h in ATT_HEAD_ORDER])
    w_xbc = seg(0)
    att_rows = w_out[SSD_WIDTH:SSD_WIDTH + ATT_WIDTH][head_perm]
    return {
        "norm_g": norm_g[None, :],
        "w_xs": w_xbc[:, :SSD_WIDTH].astype(BF16),
        "w_bc": w_xbc[:, SSD_WIDTH:].astype(BF16),
        "w_zs": seg(1).astype(BF16),
        "w_vdt": jnp.concatenate([seg(5), _pad_lanes(seg(2))], axis=1).astype(BF16),
        "w_q": seg(3)[:, head_perm].astype(BF16),
        "w_k": jnp.concatenate([seg(4)[:, :ATT_HEAD_DIM], jnp.zeros((D_MODEL, KV_WIDTH), F32),
                                seg(4)[:, ATT_HEAD_DIM:]], axis=1).astype(BF16),
        "w_za": seg(6)[:, head_perm].astype(BF16),
        "w_mq": seg(7).astype(BF16),
        "w_zm": seg(8).astype(BF16),
        "gq": jnp.tile(q_norm_g, ATT_HEADS)[None, :],
        "gk": jnp.tile(k_norm_g, K_SLAB_WIDTH // ATT_HEAD_DIM)[None, :],
        "gmq": jnp.tile(mq_norm_g, MEM_HEADS)[None, :],
        "gmk": jnp.tile(mk_norm_g, MEM_HEADS)[None, :],
        "att_bias": _attention_bias_table(),
        "dt_bias": _pad_lanes(dt_bias.reshape(1, -1)),
        "a_log": _pad_lanes(a_log.reshape(1, -1)),
        "conv_w_xs": _pad_rows(conv_w[:, :SSD_WIDTH]),
        "conv_w_bc": _pad_rows(conv_w[:, SSD_WIDTH:]),
        "conv_b_xs": conv_b[None, :SSD_WIDTH],
        "conv_b_bc": conv_b[None, SSD_WIDTH:],
        "d_skip": jnp.repeat(d_skip, SSD_HEAD_DIM)[None, :],
        "ssd_norm_g": ssd_norm_g[None, :],
        "sink": sink,
        "mem_norm_g": mem_norm_g[None, :],
        "w_mk": w_mem_kv[:, :MEM_WIDTH].astype(BF16),
        "w_mv": w_mem_kv[:, MEM_WIDTH:].astype(BF16),
        "w_out": jnp.concatenate([w_out[:SSD_WIDTH], att_rows, w_out[SSD_WIDTH + ATT_WIDTH:]], axis=0).astype(BF16),
        "e_fwd": _expand_matrix(0),
        "e_bwd": _expand_matrix(SSD_HEADS),
    }


def _layer(x, mem, p):
    batch, seq, _ = x.shape
    assert seq % CHUNK == 0 and (batch * seq) % ROW_TILE == 0
    x2d = x.reshape(batch * seq, D_MODEL)
    xs_c, bc_c, zs, q, k, v, za, mq, zm, dt = _inproj(x2d, p, seq)
    mk, mv = _memkv(mem, p)
    sb = _rev_state(xs_c, bc_c, dt, p, batch, seq)
    o = _mixer(xs_c, bc_c, zs, dt, q, k, v, za, mq, zm, sb, mk, mv, p, batch, seq)
    return _outproj(x2d, o, p["w_out"]).reshape(batch, seq, D_MODEL)


def kernel(x_prompt, x_sample, mem_prompt, mem_sample, norm_g, w_in, conv_w, conv_b, dt_bias, a_log, d_skip,
           ssd_norm_g, q_norm_g, k_norm_g, sink, mem_norm_g, w_mem_kv, mq_norm_g, mk_norm_g, w_out):
    stacked = (norm_g, w_in, conv_w, conv_b, dt_bias, a_log, d_skip, ssd_norm_g, q_norm_g, k_norm_g, sink,
               mem_norm_g, w_mem_kv, mq_norm_g, mk_norm_g, w_out)
    y_prompt, y_sample = x_prompt, x_sample
    for layer in range(norm_g.shape[0]):
        p = _prepare(*[t[layer] for t in stacked])
        y_prompt = _layer(y_prompt, mem_prompt, p)
        y_sample = _layer(y_sample, mem_sample, p)
    return (y_prompt, y_sample)
```

```python
import functools

import jax
import jax.numpy as jnp
import numpy as np
from jax import lax
from jax.experimental import pallas as pl
from jax.experimental.pallas import tpu as pltpu

F32 = jnp.float32
BF16 = jnp.bfloat16

D_MODEL = 2048
N_MEM = 256
EPS = 1e-6
SSD_WIDTH = 1024
SSD_HEAD_DIM = 64
SSD_HEADS = 16
SSD_GROUPS = 2
SSD_STATE = 128
CHUNK = 128
D_CONV = 5
ATT_WIDTH = 512
ATT_HEAD_DIM = 64
ATT_HEADS = 8
ATT_KV_HEADS = 2
WINDOW = 128
MEM_WIDTH = 512
MEM_HEADS = 4
MEM_HEAD_DIM = 128
BC_WIDTH = 2 * SSD_GROUPS * SSD_STATE
KV_WIDTH = ATT_KV_HEADS * ATT_HEAD_DIM
K_SLAB_WIDTH = 2 * KV_WIDTH
LOG2E = 1.4426950408889634
GROUP_COLS = SSD_WIDTH // SSD_GROUPS
LANES = 128
HALO = 16
ROW_TILE = 512
REV_CHUNKS = 4
MIX_CHUNKS = 2
CONV_COLS = 256
ATT_HEAD_ORDER = (0, 4, 1, 5, 2, 6, 3, 7)
ALIBI_SLOPES = tuple(2.0 ** (-8.0 * (h + 1) / ATT_HEADS) for h in range(ATT_HEADS))
VMEM_LIMIT_BIG = 52 * 1024 * 1024
VMEM_LIMIT_SMALL = 40 * 1024 * 1024


def _silu(x):
    return x * jax.nn.sigmoid(x)


def _dot(a, b):
    return jnp.dot(a, b, preferred_element_type=F32)


def _dot_nt(a, b):
    return lax.dot_general(a, b, (((1,), (1,)), ((), ())), preferred_element_type=F32)


def _split2(v):
    hi = v.astype(BF16)
    lo = (v - hi.astype(F32)).astype(BF16)
    return hi, lo


def _split3(v):
    hi = v.astype(BF16)
    r1 = v - hi.astype(F32)
    mid = r1.astype(BF16)
    lo = (r1 - mid.astype(F32)).astype(BF16)
    return hi, mid, lo


def _dot_right_exact(v, m_stacked):
    hi, lo = _split2(v)
    return _dot(jnp.concatenate([hi, lo], axis=1), m_stacked)


def _dot_left_exact(m, v):
    hi, mid, lo = _split3(v)
    wide = _dot(m, jnp.concatenate([hi, mid], axis=1))
    return wide[:, :LANES] + wide[:, LANES:] + _dot(m, lo)


def _head_rms(v, head_dim, heads_per_slab):
    lane = lax.broadcasted_iota(jnp.int32, (v.shape[0], LANES), 1)
    lo = lane < LANES // 2
    outs = []
    for s in range(v.shape[-1] // LANES):
        blk = v[:, s * LANES:(s + 1) * LANES]
        sq = blk * blk
        if heads_per_slab == 1:
            tot = jnp.sum(sq, axis=-1, keepdims=True)
        else:
            tot = jnp.where(lo, jnp.sum(jnp.where(lo, sq, 0.0), axis=-1, keepdims=True),
                            jnp.sum(jnp.where(lo, 0.0, sq), axis=-1, keepdims=True))
        outs.append(blk * lax.rsqrt(tot * (1.0 / head_dim) + EPS))
    return jnp.concatenate(outs, axis=-1)


def _softplus(x):
    return jnp.maximum(x, 0.0) + jnp.log1p(jnp.exp(-jnp.abs(x)))


def _neg_exp_alog(alog_row):
    lane = lax.broadcasted_iota(jnp.int32, alog_row.shape, 1)
    return jnp.where(lane < 2 * SSD_HEADS, -jnp.exp(alog_row), 0.0)


def _inproj_kernel(x_ref, xp_ref, xn_ref, ng_ref, wxs, wbc, wzs, wq, wk, wvdt, wza, wmq, wzm,
                   gq_ref, gk_ref, gmq_ref, dtb_ref, cwx_ref, cbx_ref, cwb_ref, cbb_ref,
                   o_xs, o_bc, o_zs, o_q, o_k, o_v, o_za, o_mq, o_zm, o_dt, hn_s, *, tiles_per_seq):
    pos = lax.rem(pl.program_id(0), tiles_per_seq)
    has_prev = pos != 0
    has_next = pos != tiles_per_seq - 1

    def normed(x):
        ms = jnp.mean(x * x, axis=-1, keepdims=True)
        return x * lax.rsqrt(ms + EPS) * ng_ref[...]

    hn_s[0:HALO, :] = jnp.where(has_prev, normed(xp_ref[...]), 0.0).astype(BF16)
    hn_s[HALO:HALO + ROW_TILE, :] = normed(x_ref[...]).astype(BF16)
    hn_s[HALO + ROW_TILE:, :] = jnp.where(has_next, normed(xn_ref[...]), 0.0).astype(BF16)
    tile_rows = slice(HALO, HALO + ROW_TILE)

    def conv_block(w_ref, cw_ref, cb_ref, o_ref, blk):
        cs = slice(blk * CONV_COLS, (blk + 1) * CONV_COLS)
        proj = _dot(hn_s[...], w_ref[:, cs])
        n_rows = proj.shape[0]
        acc = cb_ref[:, cs]
        for k in range(D_CONV):
            shifted = pltpu.roll(proj, ((D_CONV - 1) // 2 - k) % n_rows, 0) if k != (D_CONV - 1) // 2 else proj
            acc = acc + shifted[HALO:HALO + ROW_TILE] * cw_ref[k:k + 1, cs]
        o_ref[:, cs] = _silu(acc).astype(BF16)

    def gate(w_ref, o_ref):
        o_ref[...] = _silu(_dot(hn_s[tile_rows, :],w_ref[...])).astype(BF16)

    xs_blocks = [functools.partial(conv_block, wxs, cwx_ref, cbx_ref, o_xs, b)
                 for b in range(SSD_WIDTH // CONV_COLS)]
    bc_blocks = [functools.partial(conv_block, wbc, cwb_ref, cbb_ref, o_bc, b)
                 for b in range(BC_WIDTH // CONV_COLS)]
    xs_blocks[0]()
    gate(wzs, o_zs)
    xs_blocks[1]()
    q = _head_rms(_dot(hn_s[tile_rows, :],wq[...]), ATT_HEAD_DIM, 2)
    o_q[...] = (q * gq_ref[...] * (LOG2E * ATT_HEAD_DIM ** -0.5)).astype(BF16)
    xs_blocks[2]()
    k = _head_rms(_dot(hn_s[tile_rows, :],wk[...]), ATT_HEAD_DIM, 1)
    o_k[...] = (k * gk_ref[...]).astype(BF16)
    vdt = _dot(hn_s[tile_rows, :],wvdt[...])
    o_v[...] = vdt[:, :KV_WIDTH].astype(BF16)
    o_dt[...] = _softplus(vdt[:, KV_WIDTH:] + dtb_ref[...])
    xs_blocks[3]()
    gate(wza, o_za)
    bc_blocks[0]()
    mq = _head_rms(_dot(hn_s[tile_rows, :],wmq[...]), MEM_HEAD_DIM, 1)
    o_mq[...] = (mq * gmq_ref[...] * (LOG2E * MEM_HEAD_DIM ** -0.5)).astype(BF16)
    bc_blocks[1]()
    gate(wzm, o_zm)


def _const_spec(shape):
    nd = len(shape)
    return pl.BlockSpec(shape, lambda *_: (0,) * nd, pipeline_mode=pl.Buffered(1))


def _inproj(x2d, p, seq):
    rows = x2d.shape[0]
    assert rows % ROW_TILE == 0 and seq % ROW_TILE == 0
    widths = (SSD_WIDTH, BC_WIDTH, SSD_WIDTH, ATT_WIDTH, K_SLAB_WIDTH, KV_WIDTH, ATT_WIDTH, MEM_WIDTH, MEM_WIDTH)
    weights = (p["w_xs"], p["w_bc"], p["w_zs"], p["w_q"], p["w_k"], p["w_vdt"], p["w_za"], p["w_mq"], p["w_zm"])
    consts = (p["gq"], p["gk"], p["gmq"], p["dt_bias"],
              p["conv_w_xs"], p["conv_b_xs"], p["conv_w_bc"], p["conv_b_bc"])
    row_spec = lambda w: pl.BlockSpec((ROW_TILE, w), lambda i: (i, 0))
    hb = ROW_TILE // HALO
    last_hb = rows // HALO - 1
    prev_spec = pl.BlockSpec((HALO, D_MODEL), lambda i: (jnp.maximum(i * hb - 1, 0), 0))
    next_spec = pl.BlockSpec((HALO, D_MODEL), lambda i: (jnp.minimum((i + 1) * hb, last_hb), 0))
    out_shape = tuple(jax.ShapeDtypeStruct((rows, w), BF16) for w in widths) + (
        jax.ShapeDtypeStruct((rows, LANES), F32),)
    ext_rows = ROW_TILE + 2 * HALO
    return pl.pallas_call(
        functools.partial(_inproj_kernel, tiles_per_seq=seq // ROW_TILE),
        grid=(rows // ROW_TILE,),
        in_specs=[row_spec(D_MODEL), prev_spec, next_spec, _const_spec((1, D_MODEL))]
        + [_const_spec(w.shape) for w in weights] + [_const_spec(c.shape) for c in consts],
        out_specs=tuple(row_spec(w) for w in widths) + (row_spec(LANES),),
        out_shape=out_shape,
        scratch_shapes=[pltpu.VMEM((ext_rows, D_MODEL), BF16)],
        compiler_params=pltpu.CompilerParams(dimension_semantics=("arbitrary",),
                                             vmem_limit_bytes=VMEM_LIMIT_BIG),
        name="inproj",
    )(x2d, x2d, x2d, p["norm_g"], *weights, *consts)


def _memkv_kernel(mem_ref, g_ref, wk_ref, wv_ref, gmk_ref, mk_ref, mv_ref):
    m = mem_ref[0]
    ms = jnp.mean(m * m, axis=-1, keepdims=True)
    mn = (m * lax.rsqrt(ms + EPS) * g_ref[...]).astype(BF16)
    mk = _head_rms(_dot(mn, wk_ref[...]), MEM_HEAD_DIM, 1)
    mk_ref[0] = (mk * gmk_ref[...]).astype(BF16)
    mv_ref[0] = _dot(mn, wv_ref[...]).astype(BF16)


def _memkv(mem, p):
    b = mem.shape[0]
    out_spec = pl.BlockSpec((1, N_MEM, MEM_WIDTH), lambda i: (i, 0, 0))
    return pl.pallas_call(
        _memkv_kernel,
        grid=(b,),
        in_specs=[pl.BlockSpec((1, N_MEM, D_MODEL), lambda i: (i, 0, 0)), _const_spec((1, D_MODEL)),
                  _const_spec(p["w_mk"].shape), _const_spec(p["w_mv"].shape),
                  _const_spec((1, MEM_WIDTH))],
        out_specs=(out_spec, out_spec),
        out_shape=(jax.ShapeDtypeStruct((b, N_MEM, MEM_WIDTH), BF16),) * 2,
        compiler_params=pltpu.CompilerParams(dimension_semantics=("arbitrary",),
                                             vmem_limit_bytes=VMEM_LIMIT_SMALL),
        name="memkv",
    )(mem, p["mem_norm_g"], p["w_mk"], p["w_mv"], p["gmk"])


def _rev_state_kernel(xs_ref, b_ref, dt_ref, alog_ref, eb_ref, sb_ref, state):
    @pl.when(pl.program_id(1) == 0)
    def _():
        state[...] = jnp.zeros_like(state)

    neg_a = _neg_exp_alog(alog_ref[...])
    row = lax.broadcasted_iota(jnp.int32, (CHUNK, CHUNK), 0)
    col = lax.broadcasted_iota(jnp.int32, (CHUNK, CHUNK), 1)
    strict_lower = jnp.where(row > col, 1.0, 0.0).astype(BF16)
    subs = [slice(s * CHUNK, (s + 1) * CHUNK) for s in range(REV_CHUNKS)]
    weights, decays = [], []
    for rows in subs:
        dt = dt_ref[rows, :]
        a = dt * neg_a
        pre = _dot_left_exact(strict_lower, a)
        weights.append(dt * jnp.exp(pre))
        decays.append(jnp.broadcast_to(jnp.exp(jnp.sum(a, axis=0, keepdims=True)), (8, LANES)))
    scal_exp = _dot_right_exact(jnp.concatenate(weights + decays, axis=0), eb_ref[...])
    for s in reversed(range(REV_CHUNKS)):
        rows = subs[s]
        w_exp = scal_exp[rows]
        cd = scal_exp[REV_CHUNKS * CHUNK + 8 * s:REV_CHUNKS * CHUNK + 8 * s + 1]
        xw = (xs_ref[rows, :].astype(F32) * w_exp).astype(BF16)
        for g in range(SSD_GROUPS):
            gs = slice(g * GROUP_COLS, (g + 1) * GROUP_COLS)
            sb_ref[s, g] = state[g].astype(BF16)
            bt = b_ref[rows, g * SSD_STATE:(g + 1) * SSD_STATE].astype(F32).T.astype(BF16)
            state[g] = state[g] * cd[:, gs] + _dot(bt, xw[:, gs])


def _rev_state(xs_c, bc_c, dt, p, batch, seq):
    n_chunks = seq // CHUNK
    assert n_chunks % REV_CHUNKS == 0
    n_blocks = n_chunks // REV_CHUNKS
    blk_rows = REV_CHUNKS * CHUNK
    cur = lambda b, j: (b * n_blocks + (n_blocks - 1 - j), 0)
    state_shape = (SSD_GROUPS, SSD_STATE, GROUP_COLS)
    return pl.pallas_call(
        _rev_state_kernel,
        grid=(batch, n_blocks),
        in_specs=[pl.BlockSpec((blk_rows, SSD_WIDTH), cur),
                  pl.BlockSpec((blk_rows, SSD_GROUPS * SSD_STATE), cur),
                  pl.BlockSpec((blk_rows, LANES), cur), _const_spec((1, LANES)),
                  _const_spec((2 * LANES, SSD_WIDTH))],
        out_specs=pl.BlockSpec((REV_CHUNKS,) + state_shape,
                               lambda b, j: (b * n_blocks + (n_blocks - 1 - j), 0, 0, 0)),
        out_shape=jax.ShapeDtypeStruct((batch * n_chunks,) + state_shape, BF16),
        scratch_shapes=[pltpu.VMEM(state_shape, F32)],
        compiler_params=pltpu.CompilerParams(dimension_semantics=("arbitrary", "arbitrary"),
                                             vmem_limit_bytes=VMEM_LIMIT_SMALL),
        name="rev_state",
    )(xs_c, bc_c, dt, p["a_log"], p["e_bwd"])


def _mixer_kernel(xs_ref, bc_ref, zs_ref, dt_ref, q_ref, kp_ref, kc_ref, kn_ref, vp_ref, vc_ref, vn_ref,
                  za_ref, mq_ref, zm_ref, sb_ref, mk_ref, mv_ref, bias_lo_ref, bias_hi_ref, alog_ref, dskip_ref,
                  ssdg_ref, ef_ref, eb_ref, sink_ref, o_ref, state):
    @pl.when(pl.program_id(1) == 0)
    def _():
        state[...] = jnp.zeros_like(state)

    nh = SSD_HEADS
    n_slabs = ATT_WIDTH // LANES
    pairs_per_group = nh // SSD_GROUPS // 2
    mem_cols = [slice(h * MEM_HEAD_DIM, (h + 1) * MEM_HEAD_DIM) for h in range(MEM_HEADS)]
    grp_cols = [slice(g * GROUP_COLS, (g + 1) * GROUP_COLS) for g in range(SSD_GROUPS)]
    row = lax.broadcasted_iota(jnp.int32, (CHUNK, CHUNK), 0)
    col = lax.broadcasted_iota(jnp.int32, (CHUNK, CHUNK), 1)
    lane = lax.broadcasted_iota(jnp.int32, (CHUNK, LANES), 1)
    lo_half = lane < SSD_HEAD_DIM
    strict_lower = row > col
    incl_lower = jnp.where(row >= col, 1.0, 0.0).astype(BF16)
    incl_upper = jnp.where(row <= col, 1.0, 0.0).astype(BF16)
    neg_a2 = _neg_exp_alog(alog_ref[...]) * LOG2E

    def kv_blocks(prev_ref, cur_ref, next_ref, sub):
        cur = [cur_ref[s * CHUNK:(s + 1) * CHUNK, :] for s in range(MIX_CHUNKS)]
        seq = [prev_ref[...]] + cur + [next_ref[...]]
        return seq[sub:sub + 3]

    def stage1(sub):
        rows = slice(sub * CHUNK, (sub + 1) * CHUNK)
        d = {"rows": rows}
        d["dt"] = dt_ref[rows, :]
        d["a2"] = d["dt"] * neg_a2
        q_all = jnp.concatenate([q_ref[rows, s * LANES:(s + 1) * LANES] for s in range(n_slabs)], axis=0)
        k3 = kv_blocks(kp_ref, kc_ref, kn_ref, sub)
        k_all = jnp.concatenate([blk[:, g * LANES:(g + 1) * LANES] for g in range(ATT_KV_HEADS) for blk in k3],
                                axis=0)
        s_both = _dot_nt(q_all, k_all)
        d["s_att"] = [s_both[:, g * 3 * CHUNK:(g + 1) * 3 * CHUNK] for g in range(ATT_KV_HEADS)]
        d["s_mem"] = [_dot_nt(mq_ref[rows, hs], mk_ref[0, :, hs]) for hs in mem_cols]
        bc_bf = bc_ref[rows, :]
        d["b_gs"] = [bc_bf[:, g * SSD_STATE:(g + 1) * SSD_STATE] for g in range(SSD_GROUPS)]
        d["c_gs"] = [bc_bf[:, (SSD_GROUPS + g) * SSD_STATE:(SSD_GROUPS + g + 1) * SSD_STATE]
                     for g in range(SSD_GROUPS)]
        d["cb"] = [_dot_nt(d["c_gs"][g], d["b_gs"][g]) for g in range(SSD_GROUPS)]
        d["off_b"] = [_dot(d["c_gs"][g], sb_ref[sub, g]) for g in range(SSD_GROUPS)]
        d["cs_f"] = _dot_left_exact(incl_lower, d["a2"])
        d["cs_b"] = _dot_left_exact(incl_upper, d["a2"])
        return d

    def carried_state_term(d):
        d["off_f"] = [_dot(d["c_gs"][g], state[g].astype(BF16)) for g in range(SSD_GROUPS)]

    def stage2(sub, d):
        rows, dt, a2 = d["rows"], d["dt"], d["a2"]
        acs2 = jnp.where(lane < nh, d["cs_f"], d["cs_b"])
        tot2 = jnp.sum(a2, axis=0, keepdims=True)
        r2 = acs2 - jnp.log2(dt)
        acs2_t = acs2.T
        r2_t = r2.T
        ea = jnp.exp2(acs2)
        w_end = dt * jnp.exp2(tot2 - acs2)
        cd8 = jnp.broadcast_to(jnp.exp2(tot2), (8, LANES))
        cbd = [jnp.sum(d["c_gs"][g].astype(F32) * d["b_gs"][g].astype(F32), axis=-1, keepdims=True)
               for g in range(SSD_GROUPS)]
        u = dt * jnp.where(lane < nh // SSD_GROUPS, cbd[0], cbd[1])
        exp_f = _dot_right_exact(jnp.concatenate([ea, w_end, u, cd8], axis=0), ef_ref[...])
        d["ea_f"] = exp_f[0:CHUNK]
        d["w_f"] = exp_f[CHUNK:2 * CHUNK]
        d["u_f"] = exp_f[2 * CHUNK:3 * CHUNK]
        d["cd_f"] = exp_f[3 * CHUNK:3 * CHUNK + 1]
        d["ea_b"] = _dot_right_exact(ea, eb_ref[...])

        bias_ref = bias_lo_ref if sub == 0 else bias_hi_ref
        p_att, inv_att = [], []
        for g in range(ATT_KV_HEADS):
            logits = d["s_att"][g] + bias_ref[0, g]
            ps, invs = [], []
            for s in range(n_slabs):
                blk = logits[s * CHUNK:(s + 1) * CHUNK]
                sk2 = sink_ref[g * n_slabs + s] * LOG2E
                m2 = jnp.maximum(jnp.max(blk, axis=-1, keepdims=True), sk2)
                pr = jnp.exp2(blk - m2)
                den = jnp.sum(pr, axis=-1, keepdims=True) + jnp.exp2(sk2 - m2)
                ps.append(pr.astype(BF16))
                invs.append(1.0 / den)
            p_att.append(jnp.concatenate(ps, axis=0))
            inv_att.append(invs)
        d["p_att"], d["inv_att"] = p_att, inv_att
        v_cat = jnp.concatenate(kv_blocks(vp_ref, vc_ref, vn_ref, sub), axis=0).astype(F32)
        lane_kv = lax.broadcasted_iota(jnp.int32, v_cat.shape, 1)
        d["v_all"] = jnp.concatenate([jnp.where(lane_kv < ATT_HEAD_DIM, v_cat, 0.0),
                                      jnp.where(lane_kv < ATT_HEAD_DIM, 0.0, v_cat)], axis=0).astype(BF16)
        p_mem, inv_mem = [], []
        for h in range(MEM_HEADS):
            m2 = jnp.max(d["s_mem"][h], axis=-1, keepdims=True)
            pr = jnp.exp2(d["s_mem"][h] - m2)
            p_mem.append(pr.astype(BF16))
            inv_mem.append(1.0 / jnp.sum(pr, axis=-1, keepdims=True))
        d["p_mem"], d["inv_mem"] = p_mem, inv_mem

        xs = xs_ref[rows, :].astype(F32)
        d["xs"] = xs
        m_pairs, rhs_pairs = [], []
        for g in range(SSD_GROUPS):
            for pair in range(pairs_per_group):
                h0 = g * (nh // SSD_GROUPS) + 2 * pair
                ms_ = []
                for h in (h0, h0 + 1):
                    arg = jnp.where(strict_lower,
                                    acs2[:, h:h + 1] - r2_t[h:h + 1, :],
                                    acs2[:, nh + h:nh + h + 1] - r2_t[nh + h:nh + h + 1, :])
                    ms_.append((d["cb"][g] * jnp.exp2(arg)).astype(BF16))
                m_pairs.append(jnp.concatenate(ms_, axis=1))
                slab = xs[:, h0 * SSD_HEAD_DIM:(h0 + 2) * SSD_HEAD_DIM]
                rhs_pairs.append(jnp.concatenate([jnp.where(lo_half, slab, 0.0), jnp.where(lo_half, 0.0, slab)],
                                                 axis=0).astype(BF16))
        d["m_pairs"], d["rhs_pairs"] = m_pairs, rhs_pairs

    def stage3(d):
        rows, xs = d["rows"], d["xs"]
        o_att = _dot(jnp.concatenate(d["p_att"], axis=1), d["v_all"])
        o_mem = [_dot(d["p_mem"][h], mv_ref[0, :, mem_cols[h]]) for h in range(MEM_HEADS)]
        y_pairs = [_dot(m, r) for m, r in zip(d["m_pairs"], d["rhs_pairs"])]
        for g in range(SSD_GROUPS):
            gs = grp_cols[g]
            bt = d["b_gs"][g].astype(F32).T.astype(BF16)
            state[g] = state[g] * d["cd_f"][:, gs] + _dot(bt, (xs[:, gs] * d["w_f"][:, gs]).astype(BF16))

        for g in range(SSD_GROUPS):
            gs = grp_cols[g]
            y_diag = jnp.concatenate(y_pairs[g * pairs_per_group:(g + 1) * pairs_per_group], axis=1)
            y = (y_diag + d["off_f"][g] * d["ea_f"][:, gs] + d["off_b"][g] * d["ea_b"][:, gs]
                 + (dskip_ref[:, gs] + d["u_f"][:, gs]) * xs[:, gs])
            yg = y * zs_ref[rows, gs].astype(F32)
            ms = jnp.mean(yg * yg, axis=-1, keepdims=True)
            o_ref[rows, gs] = (yg * lax.rsqrt(ms + EPS) * ssdg_ref[:, gs]).astype(BF16)
        att0 = SSD_WIDTH
        for s in range(n_slabs):
            srows = slice(s * CHUNK, (s + 1) * CHUNK)
            o_slab = o_att[srows] * jnp.where(lo_half, d["inv_att"][0][s], d["inv_att"][1][s])
            o_ref[rows, att0 + s * LANES:att0 + (s + 1) * LANES] = (
                o_slab * za_ref[rows, s * LANES:(s + 1) * LANES].astype(F32)).astype(BF16)
        mem0 = SSD_WIDTH + ATT_WIDTH
        for h in range(MEM_HEADS):
            hs = mem_cols[h]
            o_ref[rows, mem0 + h * MEM_HEAD_DIM:mem0 + (h + 1) * MEM_HEAD_DIM] = (
                o_mem[h] * d["inv_mem"][h] * zm_ref[rows, hs].astype(F32)).astype(BF16)

    chunks = [stage1(sub) for sub in range(MIX_CHUNKS)]
    carried_state_term(chunks[0])
    for sub in range(MIX_CHUNKS):
        stage2(sub, chunks[sub])
    for sub in range(MIX_CHUNKS):
        if sub > 0:
            carried_state_term(chunks[sub])
        stage3(chunks[sub])


def _mixer(xs_c, bc_c, zs, dt, q, k, v, za, mq, zm, sb, mk, mv, p, batch, seq):
    n_chunks = seq // CHUNK
    assert MIX_CHUNKS == 2 and n_chunks % MIX_CHUNKS == 0
    n_blocks = n_chunks // MIX_CHUNKS
    cur = lambda b, i: (b * n_blocks + i, 0)
    prv = lambda b, i: (b * n_chunks + jnp.maximum(i * MIX_CHUNKS - 1, 0), 0)
    nxt = lambda b, i: (b * n_chunks + jnp.minimum((i + 1) * MIX_CHUNKS, n_chunks - 1), 0)
    blk = lambda w: pl.BlockSpec((MIX_CHUNKS * CHUNK, w), cur)
    halo = lambda w, im: pl.BlockSpec((CHUNK, w), im)
    state_shape = (SSD_GROUPS, SSD_STATE, GROUP_COLS)
    mem_spec = pl.BlockSpec((1, N_MEM, MEM_WIDTH), lambda b, i: (b, 0, 0))
    bias_block = (1,) + p["att_bias"].shape[1:]
    bias_lo = pl.BlockSpec(bias_block, lambda b, i: (jnp.minimum(i, 1), 0, 0, 0))
    bias_hi = pl.BlockSpec(bias_block, lambda b, i: (1 + (i + 1) // n_blocks, 0, 0, 0))
    return pl.pallas_call(
        _mixer_kernel,
        grid=(batch, n_blocks),
        in_specs=[blk(SSD_WIDTH), blk(BC_WIDTH), blk(SSD_WIDTH), blk(LANES), blk(ATT_WIDTH),
                  halo(K_SLAB_WIDTH, prv), blk(K_SLAB_WIDTH), halo(K_SLAB_WIDTH, nxt),
                  halo(KV_WIDTH, prv), blk(KV_WIDTH), halo(KV_WIDTH, nxt),
                  blk(ATT_WIDTH), blk(MEM_WIDTH), blk(MEM_WIDTH),
                  pl.BlockSpec((MIX_CHUNKS,) + state_shape, lambda b, i: (b * n_blocks + i, 0, 0, 0)),
                  mem_spec, mem_spec, bias_lo, bias_hi,
                  _const_spec((1, LANES)), _const_spec((1, SSD_WIDTH)), _const_spec((1, SSD_WIDTH)),
                  _const_spec((2 * LANES, SSD_WIDTH)), _const_spec((2 * LANES, SSD_WIDTH)),
                  pl.BlockSpec(memory_space=pltpu.SMEM)],
        out_specs=blk(D_MODEL),
        out_shape=jax.ShapeDtypeStruct((batch * seq, D_MODEL), BF16),
        scratch_shapes=[pltpu.VMEM(state_shape, F32)],
        compiler_params=pltpu.CompilerParams(dimension_semantics=("arbitrary", "arbitrary"),
                                             vmem_limit_bytes=VMEM_LIMIT_SMALL),
        name="mixer",
    )(xs_c, bc_c, zs, dt, q, k, k, k, v, v, v, za, mq, zm, sb, mk, mv, p["att_bias"], p["att_bias"],
      p["a_log"], p["d_skip"], p["ssd_norm_g"], p["e_fwd"], p["e_bwd"], p["sink"])


def _outproj_kernel(x_ref, o_ref, w_ref, y_ref):
    y_ref[...] = x_ref[...] + _dot(o_ref[...], w_ref[...])


def _outproj(x2d, o2d, w_out):
    rows = x2d.shape[0]
    row_spec = pl.BlockSpec((ROW_TILE, D_MODEL), lambda i: (i, 0))
    return pl.pallas_call(
        _outproj_kernel,
        grid=(rows // ROW_TILE,),
        in_specs=[row_spec, row_spec, _const_spec((D_MODEL, D_MODEL))],
        out_specs=row_spec,
        out_shape=jax.ShapeDtypeStruct((rows, D_MODEL), F32),
        compiler_params=pltpu.CompilerParams(dimension_semantics=("arbitrary",),
                                             vmem_limit_bytes=VMEM_LIMIT_SMALL),
        name="outproj",
    )(x2d, o2d, w_out)


def _expand_matrix(lane_offset):
    r = np.arange(2 * LANES)[:, None] % LANES
    c = np.arange(SSD_WIDTH)[None, :] // SSD_HEAD_DIM
    return np.where(r == c + lane_offset, 1.0, 0.0).astype(BF16)


def _attention_bias_table():
    qi = np.arange(CHUNK)[:, None]
    kj = np.arange(3 * CHUNK)[None, :]
    dist = np.abs(qi - (kj - CHUNK))
    in_band = dist <= WINDOW
    edge_valid = (kj >= CHUNK, kj >= 0, kj < 2 * CHUNK)
    per_kv = ATT_HEADS // ATT_KV_HEADS
    variants = []
    for ok in edge_valid:
        groups = []
        for g in range(ATT_KV_HEADS):
            heads = [np.where(in_band & ok, np.float32(-LOG2E * ALIBI_SLOPES[g * per_kv + s]) * dist.astype(np.float32),
                              np.float32(-np.inf)) for s in range(per_kv)]
            groups.append(np.concatenate(heads, axis=0))
        variants.append(np.stack(groups))
    return np.stack(variants).astype(np.float32)


def _pad_lanes(v, width=LANES):
    return jnp.pad(v, ((0, 0), (0, width - v.shape[-1])))


def _pad_rows(v, rows=8):
    return jnp.pad(v, ((0, rows - v.shape[0]), (0, 0)))


def _prepare(norm_g, w_in, conv_w, conv_b, dt_bias, a_log, d_skip, ssd_norm_g, q_norm_g, k_norm_g, sink,
             mem_norm_g, w_mem_kv, mq_norm_g, mk_norm_g, w_out):
    sizes = (SSD_WIDTH + BC_WIDTH, SSD_WIDTH, 2 * SSD_HEADS, ATT_WIDTH, KV_WIDTH, KV_WIDTH, ATT_WIDTH,
             MEM_WIDTH, MEM_WIDTH)
    offs = [0]
    for s in sizes:
        offs.append(offs[-1] + s)
    seg = lambda i: w_in[:, offs[i]:offs[i + 1]]
    per_kv = ATT_HEADS // ATT_KV_HEADS

    def slab_order_cols(w):
        return w.reshape(-1, ATT_KV_HEADS, per_kv, ATT_HEAD_DIM).swapaxes(1, 2).reshape(-1, ATT_WIDTH)

    w_xbc = seg(0)
    att_rows = w_out[SSD_WIDTH:SSD_WIDTH + ATT_WIDTH].reshape(ATT_KV_HEADS, per_kv, ATT_HEAD_DIM, D_MODEL)
    att_rows = att_rows.swapaxes(0, 1).reshape(ATT_WIDTH, D_MODEL)
    return {
        "norm_g": norm_g[None, :],
        "w_xs": w_xbc[:, :SSD_WIDTH].astype(BF16),
        "w_bc": w_xbc[:, SSD_WIDTH:].astype(BF16),
        "w_zs": seg(1).astype(BF16),
        "w_vdt": jnp.concatenate([seg(5), _pad_lanes(seg(2))], axis=1).astype(BF16),
        "w_q": slab_order_cols(seg(3)).astype(BF16),
        "w_k": jnp.concatenate([seg(4)[:, :ATT_HEAD_DIM], jnp.zeros((D_MODEL, KV_WIDTH), F32),
                                seg(4)[:, ATT_HEAD_DIM:]], axis=1).astype(BF16),
        "w_za": slab_order_cols(seg(6)).astype(BF16),
        "w_mq": seg(7).astype(BF16),
        "w_zm": seg(8).astype(BF16),
        "gq": jnp.tile(q_norm_g, ATT_HEADS)[None, :],
        "gk": jnp.tile(k_norm_g, K_SLAB_WIDTH // ATT_HEAD_DIM)[None, :],
        "gmq": jnp.tile(mq_norm_g, MEM_HEADS)[None, :],
        "gmk": jnp.tile(mk_norm_g, MEM_HEADS)[None, :],
        "att_bias": _attention_bias_table(),
        "dt_bias": _pad_lanes(dt_bias.reshape(1, -1)),
        "a_log": _pad_lanes(a_log.reshape(1, -1)),
        "conv_w_xs": _pad_rows(conv_w[:, :SSD_WIDTH]),
        "conv_w_bc": _pad_rows(conv_w[:, SSD_WIDTH:]),
        "conv_b_xs": conv_b[None, :SSD_WIDTH],
        "conv_b_bc": conv_b[None, SSD_WIDTH:],
        "d_skip": jnp.repeat(d_skip, SSD_HEAD_DIM)[None, :],
        "ssd_norm_g": ssd_norm_g[None, :],
        "sink": sink,
        "mem_norm_g": mem_norm_g[None, :],
        "w_mk": w_mem_kv[:, :MEM_WIDTH].astype(BF16),
        "w_mv": w_mem_kv[:, MEM_WIDTH:].astype(BF16),
        "w_out": jnp.concatenate([w_out[:SSD_WIDTH], att_rows, w_out[SSD_WIDTH + ATT_WIDTH:]], axis=0).astype(BF16),
        "e_fwd": _expand_matrix(0),
        "e_bwd": _expand_matrix(SSD_HEADS),
    }


def _layer(x, mem, p):
    batch, seq, _ = x.shape
    assert seq % CHUNK == 0 and (batch * seq) % ROW_TILE == 0
    x2d = x.reshape(batch * seq, D_MODEL)
    xs_c, bc_c, zs, q, k, v, za, mq, zm, dt = _inproj(x2d, p, seq)
    mk, mv = _memkv(mem, p)
    sb = _rev_state(xs_c, bc_c, dt, p, batch, seq)
    o = _mixer(xs_c, bc_c, zs, dt, q, k, v, za, mq, zm, sb, mk, mv, p, batch, seq)
    return _outproj(x2d, o, p["w_out"]).reshape(batch, seq, D_MODEL)


def kernel(x_prompt, x_sample, mem_prompt, mem_sample, norm_g, w_in, conv_w, conv_b, dt_bias, a_log, d_skip,
           ssd_norm_g, q_norm_g, k_norm_g, sink, mem_norm_g, w_mem_kv, mq_norm_g, mk_norm_g, w_out):
    stacked = (norm_g, w_in, conv_w, conv_b, dt_bias, a_log, d_skip, ssd_norm_g, q_norm_g, k_norm_g, sink,
               mem_norm_g, w_mem_kv, mq_norm_g, mk_norm_g, w_out)
    y_prompt, y_sample = x_prompt, x_sample
    for layer in range(norm_g.shape[0]):
        p = _prepare(*[t[layer] for t in stacked])
        y_prompt = _layer(y_prompt, mem_prompt, p)
        y_sample = _layer(y_sample, mem_sample, p)
    return (y_prompt, y_sample)
```
